```python
import math
import jax, jax.numpy as jnp
from jax import lax
import numpy as np

D_MODEL = 2048
BATCH = 8
SEQ = 2048
DEPTH = 4

N_MIXERS = 2
N_SSD_LAYERS = (DEPTH + 1) // 2
N_SB_LAYERS = DEPTH // 2
PLE_DIM = 256
SSD_EXPAND = 2
SSD_D_INNER = SSD_EXPAND * D_MODEL
SSD_HEAD_DIM = 64
SSD_N_HEADS = SSD_D_INNER // SSD_HEAD_DIM
SSD_N_GROUPS = 8
SSD_HEADS_PER_GROUP = SSD_N_HEADS // SSD_N_GROUPS
SSD_D_STATE = 128
SSD_D_CONV = 4
SSD_CHUNK = 128
SSD_CONV_DIM = SSD_D_INNER + 2 * SSD_N_GROUPS * SSD_D_STATE
SSD_IN_DIM = SSD_D_INNER + SSD_CONV_DIM + SSD_N_HEADS
SB_HEAD_DIM = 128
SB_N_HEADS = D_MODEL // SB_HEAD_DIM
SB_WIDTH = SB_N_HEADS * SB_HEAD_DIM
SB_QBLOCK = 128
NORM_EPS = 1e-6
GATED_NORM_EPS = 1e-5

kernel_name = "ssd_stickbreaking_interleaved_ple"


def rms_norm(x, w, eps=NORM_EPS):
    xf = x.astype(jnp.float32)
    y = xf * lax.rsqrt(jnp.mean(xf * xf, axis=-1, keepdims=True) + eps)
    return (y * w.astype(jnp.float32)).astype(x.dtype)


def causal_depthwise_conv(x, w, b):
    y = lax.conv_general_dilated(
        x, w[:, None, :], window_strides=(1,), padding=[(SSD_D_CONV - 1, 0)],
        dimension_numbers=("NWC", "WIO", "NWC"), feature_group_count=x.shape[-1])
    return y + b


def ssd_chunked_scan(xh, dt, a, bm, cm):
    b, s = xh.shape[0], xh.shape[1]
    nc = s // SSD_CHUNK
    L = SSD_CHUNK
    xdt = (xh * dt[..., None]).reshape(b, nc, L, SSD_N_GROUPS, SSD_HEADS_PER_GROUP, SSD_HEAD_DIM)
    adt = (dt * a).reshape(b, nc, L, SSD_N_GROUPS, SSD_HEADS_PER_GROUP)
    bm = bm.reshape(b, nc, L, SSD_N_GROUPS, SSD_D_STATE)
    cm = cm.reshape(b, nc, L, SSD_N_GROUPS, SSD_D_STATE)
    acum = jnp.cumsum(adt, axis=2)
    seg = acum[:, :, :, None] - acum[:, :, None, :]
    causal = jnp.tril(jnp.ones((L, L), dtype=bool))[None, None, :, :, None, None]
    decay = jnp.exp(jnp.where(causal, seg, -jnp.inf))
    scores = jnp.einsum("bclgn,bcsgn->bclsg", cm, bm)
    y_diag = jnp.einsum("bclsg,bclsgr,bcsgrp->bclgrp", scores, decay, xdt)
    decay_to_end = jnp.exp(acum[:, :, -1:] - acum)
    chunk_states = jnp.einsum("bclgn,bclgr,bclgrp->bcgrpn", bm, decay_to_end, xdt)
    chunk_decay = jnp.exp(acum[:, :, -1])

    def step(state, inp):
        cs, cd = inp
        return state * cd[..., None, None] + cs, state

    init = jnp.zeros((b, SSD_N_GROUPS, SSD_HEADS_PER_GROUP, SSD_HEAD_DIM, SSD_D_STATE), jnp.float32)
    _, prev_states = lax.scan(step, init, (jnp.moveaxis(chunk_states, 1, 0), jnp.moveaxis(chunk_decay, 1, 0)))
    prev_states = jnp.moveaxis(prev_states, 0, 1)
    y_off = jnp.einsum("bclgn,bcgrpn,bclgr->bclgrp", cm, prev_states, jnp.exp(acum))
    return (y_diag + y_off).reshape(b, s, SSD_N_GROUPS, SSD_HEADS_PER_GROUP, SSD_HEAD_DIM)


def ssd_branch(u, in_w, conv_w, conv_b, dt_bias, a_log, d_skip, gnorm_w, out_w):
    b, s, _ = u.shape
    proj = u @ in_w
    z = proj[..., :SSD_D_INNER]
    xbc = proj[..., SSD_D_INNER:SSD_D_INNER + SSD_CONV_DIM]
    dt_raw = proj[..., SSD_D_INNER + SSD_CONV_DIM:]
    xbc = jax.nn.silu(causal_depthwise_conv(xbc, conv_w, conv_b))
    nbc = SSD_N_GROUPS * SSD_D_STATE
    xs = xbc[..., :SSD_D_INNER].astype(jnp.float32).reshape(b, s, SSD_N_GROUPS, SSD_HEADS_PER_GROUP, SSD_HEAD_DIM)
    bm = xbc[..., SSD_D_INNER:SSD_D_INNER + nbc].astype(jnp.float32).reshape(b, s, SSD_N_GROUPS, SSD_D_STATE)
    cm = xbc[..., SSD_D_INNER + nbc:].astype(jnp.float32).reshape(b, s, SSD_N_GROUPS, SSD_D_STATE)
    dt = jax.nn.softplus(dt_raw.astype(jnp.float32) + dt_bias.astype(jnp.float32))
    dt = dt.reshape(b, s, SSD_N_GROUPS, SSD_HEADS_PER_GROUP)
    a = (-jnp.exp(a_log.astype(jnp.float32))).reshape(SSD_N_GROUPS, SSD_HEADS_PER_GROUP)
    y = ssd_chunked_scan(xs, dt, a, bm, cm)
    y = y + d_skip.astype(jnp.float32).reshape(SSD_N_GROUPS, SSD_HEADS_PER_GROUP)[..., None] * xs
    y = y.reshape(b, s, SSD_D_INNER) * jax.nn.silu(z.astype(jnp.float32))
    yg = y.reshape(b, s, SSD_N_GROUPS, SSD_D_INNER // SSD_N_GROUPS)
    yg = yg * lax.rsqrt(jnp.mean(yg * yg, axis=-1, keepdims=True) + GATED_NORM_EPS)
    y = (yg.reshape(b, s, SSD_D_INNER) * gnorm_w.astype(jnp.float32)).astype(u.dtype)
    return y @ out_w


def stick_breaking_attention(q, k, v):
    s = q.shape[2]
    scale = 1.0 / math.sqrt(SB_HEAD_DIM)
    outs = []
    for blk in range(s // SB_QBLOCK):
        t0 = blk * SB_QBLOCK
        kend = t0 + SB_QBLOCK
        z = jnp.einsum("bhtd,bhsd->bhts", q[:, :, t0:kend], k[:, :, :kend]) * scale
        t_idx = t0 + jnp.arange(SB_QBLOCK)[:, None]
        s_idx = jnp.arange(kend)[None, :]
        strict = s_idx < t_idx
        log_beta = jax.nn.log_sigmoid(z)
        log_1m_beta = jnp.where(strict, jax.nn.log_sigmoid(-z), 0.0)
        rest = lax.cumsum(log_1m_beta, axis=3, reverse=True) - log_1m_beta
        att = jnp.where(strict, jnp.exp(log_beta + rest), 0.0)
        outs.append(jnp.einsum("bhts,bhsd->bhtd", att, v[:, :, :kend]))
    return jnp.concatenate(outs, axis=2)


def sb_branch(u, in_w, qn_w, kn_w, out_w):
    b, s, _ = u.shape
    proj = u @ in_w
    q, k, v, g = jnp.split(proj, 4, axis=-1)
    def heads(t):
        return t.reshape(b, s, SB_N_HEADS, SB_HEAD_DIM)
    q = rms_norm(heads(q), qn_w).astype(jnp.float32).transpose(0, 2, 1, 3)
    k = rms_norm(heads(k), kn_w).astype(jnp.float32).transpose(0, 2, 1, 3)
    v = heads(v).astype(jnp.float32).transpose(0, 2, 1, 3)
    o = stick_breaking_attention(q, k, v).transpose(0, 2, 1, 3).reshape(b, s, SB_WIDTH)
    o = (o * jax.nn.silu(g.astype(jnp.float32))).astype(u.dtype)
    return o @ out_w


def setup_inputs(seed: int = 0) -> dict:
    key = jax.random.key(seed)
    ks = jax.random.split(key, 24)
    f32 = jnp.float32
    nrm = lambda k, shape, sc: jax.random.normal(k, shape, f32) * sc
    dt0 = jnp.exp(jax.random.uniform(ks[7], (N_SSD_LAYERS, SSD_N_HEADS), f32)
                  * (math.log(0.1) - math.log(0.001)) + math.log(0.001))
    return {
        "x": nrm(ks[0], (BATCH, SEQ, D_MODEL), 1.0),
        "p": nrm(ks[1], (DEPTH, BATCH, SEQ, PLE_DIM), 1.0),
        "norm_w": 1.0 + nrm(ks[2], (DEPTH, D_MODEL), 0.02),
        "ssd_in_w": nrm(ks[3], (N_SSD_LAYERS, D_MODEL, SSD_IN_DIM), D_MODEL ** -0.5),
        "ssd_conv_w": nrm(ks[4], (N_SSD_LAYERS, SSD_D_CONV, SSD_CONV_DIM), SSD_D_CONV ** -0.5),
        "ssd_conv_b": nrm(ks[5], (N_SSD_LAYERS, SSD_CONV_DIM), 0.02),
        "ssd_dt_bias": dt0 + jnp.log(-jnp.expm1(-dt0)),
        "ssd_a_log": jnp.log(jax.random.uniform(ks[8], (N_SSD_LAYERS, SSD_N_HEADS), f32, 1.0, 16.0)),
        "ssd_d": 1.0 + nrm(ks[9], (N_SSD_LAYERS, SSD_N_HEADS), 0.02),
        "ssd_gnorm_w": 1.0 + nrm(ks[10], (N_SSD_LAYERS, SSD_D_INNER), 0.02),
        "ssd_out_w": nrm(ks[11], (N_SSD_LAYERS, SSD_D_INNER, D_MODEL), SSD_D_INNER ** -0.5),
        "sb_in_w": nrm(ks[12], (N_SB_LAYERS, D_MODEL, 4 * SB_WIDTH), D_MODEL ** -0.5),
        "sb_qn_w": 1.0 + nrm(ks[13], (N_SB_LAYERS, SB_HEAD_DIM), 0.02),
        "sb_kn_w": 1.0 + nrm(ks[14], (N_SB_LAYERS, SB_HEAD_DIM), 0.02),
        "sb_out_w": nrm(ks[15], (N_SB_LAYERS, SB_WIDTH, D_MODEL), SB_WIDTH ** -0.5),
        "ple_norm_w": 1.0 + nrm(ks[16], (DEPTH, D_MODEL), 0.02),
        "ple_gate_w": nrm(ks[17], (DEPTH, D_MODEL, D_MODEL), D_MODEL ** -0.5),
        "ple_proj_w": nrm(ks[18], (DEPTH, PLE_DIM, D_MODEL), 0.5 * PLE_DIM ** -0.5),
    }


def reference(x, p, norm_w, ssd_in_w, ssd_conv_w, ssd_conv_b, ssd_dt_bias, ssd_a_log, ssd_d,
              ssd_gnorm_w, ssd_out_w, sb_in_w, sb_qn_w, sb_kn_w, sb_out_w,
              ple_norm_w, ple_gate_w, ple_proj_w):
    h = x
    for i in range(DEPTH):
        u = rms_norm(h, norm_w[i])
        j = i // N_MIXERS
        if i % N_MIXERS == 0:
            mix = ssd_branch(u, ssd_in_w[j], ssd_conv_w[j], ssd_conv_b[j], ssd_dt_bias[j],
                             ssd_a_log[j], ssd_d[j], ssd_gnorm_w[j], ssd_out_w[j])
        else:
            mix = sb_branch(u, sb_in_w[j], sb_qn_w[j], sb_kn_w[j], sb_out_w[j])
        h = h + mix
        gate = jax.nn.sigmoid((rms_norm(h, ple_norm_w[i]) @ ple_gate_w[i]).astype(jnp.float32))
        h = h + ((p[i] @ ple_proj_w[i]).astype(jnp.float32) * gate).astype(h.dtype)
    return h
```

```python
import functools
import math

import jax
import jax.numpy as jnp
from jax import lax
from jax.experimental import pallas as pl
from jax.experimental.pallas import tpu as pltpu

F32 = jnp.float32
BF16 = jnp.bfloat16

NORM_EPS = 1e-6
GATED_NORM_EPS = 1e-5

SSD_HEAD_DIM = 64
SSD_N_GROUPS = 8
SSD_D_STATE = 128
SSD_D_CONV = 4
SSD_CHUNK = 128
SB_HEAD_DIM = 128

LANES = 128
SUBLANES = 8
V7X_VMEM_BYTES = 64 * 1024 * 1024
COMPILER_SCRATCH_BYTES = 12 * 1024 * 1024


def _vmem_limit(block_bytes, scratch_bytes=0):
    need = 2 * sum(block_bytes) + scratch_bytes + COMPILER_SCRATCH_BYTES
    return int(min(need, V7X_VMEM_BYTES - 8 * 1024 * 1024))


def _nbytes(shape, dtype):
    return math.prod(shape) * jnp.dtype(dtype).itemsize


def _pick_tile(n, pref):
    t = min(n, pref)
    assert n % t == 0, (n, t)
    return t


def _sigmoid(x):
    return 1.0 / (1.0 + jnp.exp(-x))


def _softplus(x):
    return jnp.maximum(x, 0.0) + jnp.log(1.0 + jnp.exp(-jnp.abs(x)))


def _rms_normalize(x, w, eps):
    ms = jnp.mean(x * x, axis=-1, keepdims=True)
    return x * lax.rsqrt(ms + eps) * w


def _norm_matmul_kernel(x_ref, nw_ref, w_ref, o_ref, xn_ref):
    @pl.when(pl.program_id(1) == 0)
    def _():
        xn_ref[...] = _rms_normalize(x_ref[...], nw_ref[...], NORM_EPS).astype(xn_ref.dtype)

    o_ref[...] = jnp.dot(xn_ref[...], w_ref[...], preferred_element_type=F32).astype(o_ref.dtype)


def _norm_matmul(x, nw, w, out_dtype, tm_pref=1024, tn_pref=512):
    t, d = x.shape
    n = w.shape[1]
    tm, tn = _pick_tile(t, tm_pref), _pick_tile(n, tn_pref)
    blocks = [_nbytes((tm, d), F32), _nbytes((d, tn), BF16), _nbytes((tm, tn), out_dtype)]
    return pl.pallas_call(
        _norm_matmul_kernel,
        grid=(t // tm, n // tn),
        in_specs=[
            pl.BlockSpec((tm, d), lambda i, j: (i, 0)),
            pl.BlockSpec((1, d), lambda i, j: (0, 0)),
            pl.BlockSpec((d, tn), lambda i, j: (0, j)),
        ],
        out_specs=pl.BlockSpec((tm, tn), lambda i, j: (i, j)),
        out_shape=jax.ShapeDtypeStruct((t, n), out_dtype),
        scratch_shapes=[pltpu.VMEM((tm, d), BF16)],
        compiler_params=pltpu.CompilerParams(
            dimension_semantics=("parallel", "arbitrary"),
            vmem_limit_bytes=_vmem_limit(blocks, _nbytes((tm, d), BF16))),
        name="norm_matmul",
    )(x, nw.reshape(1, d), w)


def _sb_in_proj_kernel(x_ref, nw_ref, w_ref, qn_ref, kn_ref, o_ref, xn_ref, *, tiles_per_section):
    j = pl.program_id(1)

    @pl.when(j == 0)
    def _():
        xn_ref[...] = _rms_normalize(x_ref[...], nw_ref[...], NORM_EPS).astype(xn_ref.dtype)

    acc = jnp.dot(xn_ref[...], w_ref[...], preferred_element_type=F32)
    section = j // tiles_per_section
    heads_per_tile = acc.shape[1] // SB_HEAD_DIM

    def head_norm(hw_ref, scale):
        for hh in range(heads_per_tile):
            sl = slice(hh * SB_HEAD_DIM, (hh + 1) * SB_HEAD_DIM)
            y = _rms_normalize(acc[:, sl], hw_ref[...], NORM_EPS)
            o_ref[:, sl] = (y * scale).astype(o_ref.dtype)

    @pl.when(section == 0)
    def _():
        head_norm(qn_ref, 1.0 / math.sqrt(SB_HEAD_DIM))

    @pl.when(section == 1)
    def _():
        head_norm(kn_ref, 1.0)

    @pl.when(section == 2)
    def _():
        o_ref[...] = acc.astype(o_ref.dtype)

    @pl.when(section == 3)
    def _():
        o_ref[...] = (acc * _sigmoid(acc)).astype(o_ref.dtype)


def _sb_in_proj(x, nw, w, qn_w, kn_w, tm_pref=1024, tn_pref=512):
    t, d = x.shape
    n = w.shape[1]
    width = n // 4
    tm, tn = _pick_tile(t, tm_pref), _pick_tile(width, tn_pref)
    blocks = [_nbytes((tm, d), F32), _nbytes((d, tn), BF16), _nbytes((tm, tn), BF16)]
    return pl.pallas_call(
        functools.partial(_sb_in_proj_kernel, tiles_per_section=width // tn),
        grid=(t // tm, n // tn),
        in_specs=[
            pl.BlockSpec((tm, d), lambda i, j: (i, 0)),
            pl.BlockSpec((1, d), lambda i, j: (0, 0)),
            pl.BlockSpec((d, tn), lambda i, j: (0, j)),
            pl.BlockSpec((1, SB_HEAD_DIM), lambda i, j: (0, 0)),
            pl.BlockSpec((1, SB_HEAD_DIM), lambda i, j: (0, 0)),
        ],
        out_specs=pl.BlockSpec((tm, tn), lambda i, j: (i, j)),
        out_shape=jax.ShapeDtypeStruct((t, n), BF16),
        scratch_shapes=[pltpu.VMEM((tm, d), BF16)],
        compiler_params=pltpu.CompilerParams(
            dimension_semantics=("parallel", "arbitrary"),
            vmem_limit_bytes=_vmem_limit(blocks, _nbytes((tm, d), BF16))),
        name="sb_in_proj",
    )(x, nw.reshape(1, d), w, qn_w.reshape(1, SB_HEAD_DIM), kn_w.reshape(1, SB_HEAD_DIM))


def _matmul_residual_kernel(a_ref, w_ref, h_ref, o_ref):
    o_ref[...] = h_ref[...] + jnp.dot(a_ref[...], w_ref[...], preferred_element_type=F32)


def _matmul_residual(a, w, h, tm_pref=1024, tn_pref=512):
    t, k = a.shape
    n = w.shape[1]
    tm, tn = _pick_tile(t, tm_pref), _pick_tile(n, tn_pref)
    blocks = [_nbytes((tm, k), BF16), _nbytes((k, tn), BF16), 2 * _nbytes((tm, tn), F32)]
    return pl.pallas_call(
        _matmul_residual_kernel,
        grid=(t // tm, n // tn),
        in_specs=[
            pl.BlockSpec((tm, k), lambda i, j: (i, 0)),
            pl.BlockSpec((k, tn), lambda i, j: (0, j)),
            pl.BlockSpec((tm, tn), lambda i, j: (i, j)),
        ],
        out_specs=pl.BlockSpec((tm, tn), lambda i, j: (i, j)),
        out_shape=jax.ShapeDtypeStruct((t, n), F32),
        compiler_params=pltpu.CompilerParams(
            dimension_semantics=("parallel", "arbitrary"),
            vmem_limit_bytes=_vmem_limit(blocks)),
        name="matmul_residual",
    )(a, w, h)


def _ple_kernel(h_ref, nw_ref, gw_ref, p_ref, pw_ref, hres_ref, o_ref, xn_ref):
    @pl.when(pl.program_id(1) == 0)
    def _():
        xn_ref[...] = _rms_normalize(h_ref[...], nw_ref[...], NORM_EPS).astype(xn_ref.dtype)

    gate = _sigmoid(jnp.dot(xn_ref[...], gw_ref[...], preferred_element_type=F32))
    emb = jnp.dot(p_ref[...].astype(BF16), pw_ref[...], preferred_element_type=F32)
    o_ref[...] = hres_ref[...] + emb * gate


def _ple(h, nw, gw, p, pw, tm_pref=1024, tn_pref=512):
    t, d = h.shape
    pd = p.shape[1]
    tm, tn = _pick_tile(t, tm_pref), _pick_tile(d, tn_pref)
    blocks = [_nbytes((tm, d), F32), _nbytes((d, tn), BF16), _nbytes((tm, pd), F32),
              _nbytes((pd, tn), BF16), 2 * _nbytes((tm, tn), F32)]
    return pl.pallas_call(
        _ple_kernel,
        grid=(t // tm, d // tn),
        in_specs=[
            pl.BlockSpec((tm, d), lambda i, j: (i, 0)),
            pl.BlockSpec((1, d), lambda i, j: (0, 0)),
            pl.BlockSpec((d, tn), lambda i, j: (0, j)),
            pl.BlockSpec((tm, pd), lambda i, j: (i, 0)),
            pl.BlockSpec((pd, tn), lambda i, j: (0, j)),
            pl.BlockSpec((tm, tn), lambda i, j: (i, j)),
        ],
        out_specs=pl.BlockSpec((tm, tn), lambda i, j: (i, j)),
        out_shape=jax.ShapeDtypeStruct((t, d), F32),
        scratch_shapes=[pltpu.VMEM((tm, d), BF16)],
        compiler_params=pltpu.CompilerParams(
            dimension_semantics=("parallel", "arbitrary"),
            vmem_limit_bytes=_vmem_limit(blocks, _nbytes((tm, d), BF16))),
        name="ple",
    )(h, nw.reshape(1, d), gw, p, pw, h)


def _sb_attn_kernel(q_ref, k_ref, v_ref, g_ref, o_ref, *, blk):
    i = pl.program_id(2)
    q = q_ref[...]
    row = lax.broadcasted_iota(jnp.int32, (blk, blk), 0)
    col = lax.broadcasted_iota(jnp.int32, (blk, blk), 1)
    strict = col < row
    later = (row > col).astype(BF16)

    def block(j, carry, acc, diagonal):
        start = pl.multiple_of(j * blk, blk)
        kj = k_ref[pl.ds(start, blk), :]
        vj = v_ref[pl.ds(start, blk), :]
        z = lax.dot_general(q, kj, (((1,), (1,)), ((), ())), preferred_element_type=F32)
        sp = _softplus(z)
        l1m = -sp
        if diagonal:
            l1m = jnp.where(strict, l1m, 0.0)
        hi = l1m.astype(BF16)
        lo = (l1m - hi.astype(F32)).astype(BF16)
        rest = (jnp.dot(hi, later, preferred_element_type=F32)
                + jnp.dot(lo, later, preferred_element_type=F32) + carry)
        att = jnp.exp((z - sp) + rest)
        if diagonal:
            att = jnp.where(strict, att, 0.0)
        acc = acc + jnp.dot(att.astype(BF16), vj, preferred_element_type=F32)
        carry = carry + jnp.sum(l1m, axis=-1, keepdims=True)
        return carry, acc

    carry = jnp.zeros((blk, 1), F32)
    acc = jnp.zeros((blk, SB_HEAD_DIM), F32)
    carry, acc = block(i, carry, acc, True)

    def body(t, c):
        return block(i - 1 - t, c[0], c[1], False)

    carry, acc = lax.fori_loop(0, i, body, (carry, acc))
    o_ref[...] = (acc * g_ref[...].astype(F32)).astype(o_ref.dtype)


def _sb_attention(qkvg, batch, seq, blk_pref=256):
    t, n = qkvg.shape
    heads = n // (4 * SB_HEAD_DIM)
    blk = _pick_tile(seq, blk_pref)
    nq = seq // blk
    hd = SB_HEAD_DIM
    blocks = [_nbytes((blk, hd), BF16), 2 * _nbytes((seq, hd), BF16), 2 * _nbytes((blk, hd), BF16)]
    return pl.pallas_call(
        functools.partial(_sb_attn_kernel, blk=blk),
        grid=(batch, heads, nq),
        in_specs=[
            pl.BlockSpec((blk, hd), lambda b, h, i: (b * nq + i, h)),
            pl.BlockSpec((seq, hd), lambda b, h, i: (b, heads + h)),
            pl.BlockSpec((seq, hd), lambda b, h, i: (b, 2 * heads + h)),
            pl.BlockSpec((blk, hd), lambda b, h, i: (b * nq + i, 3 * heads + h)),
        ],
        out_specs=pl.BlockSpec((blk, hd), lambda b, h, i: (b * nq + i, h)),
        out_shape=jax.ShapeDtypeStruct((t, heads * hd), BF16),
        compiler_params=pltpu.CompilerParams(
            dimension_semantics=("parallel", "parallel", "arbitrary"),
            vmem_limit_bytes=_vmem_limit(blocks)),
        name="sb_attention",
    )(qkvg, qkvg, qkvg, qkvg)


def _split3(v):
    hi = v.astype(BF16)
    r = v - hi.astype(F32)
    mid = r.astype(BF16)
    lo = (r - mid.astype(F32)).astype(BF16)
    return hi, mid, lo


def _ssd_kernel(z_ref, x_ref, b_ref, c_ref, dt_ref, spread_ref, convw_ref, convb_ref, dtb_ref,
                alog_ref, dexp_ref, gnw_ref, o_ref,
                cbuf, state_ref, acum_sp_ref, acumt_ref, dtt_ref, wt_ref, *, d_inner):
    chunk = SSD_CHUNK
    heads_per_group = d_inner // SSD_N_GROUPS // SSD_HEAD_DIM
    gw = d_inner // SSD_N_GROUPS
    nbc = SSD_N_GROUPS * SSD_D_STATE
    first = pl.program_id(1) == 0

    @pl.when(first)
    def _():
        cbuf[0:SUBLANES, :] = jnp.zeros((SUBLANES, cbuf.shape[1]), F32)
        state_ref[...] = jnp.zeros(state_ref.shape, F32)

    cbuf[SUBLANES:SUBLANES + chunk, 0:d_inner] = x_ref[...]
    cbuf[SUBLANES:SUBLANES + chunk, d_inner:d_inner + nbc] = b_ref[...]
    cbuf[SUBLANES:SUBLANES + chunk, d_inner + nbc:d_inner + 2 * nbc] = c_ref[...]

    dt = _softplus(dt_ref[...] + dtb_ref[...])
    adt = dt * (-jnp.exp(alog_ref[...]))
    li = lax.broadcasted_iota(jnp.int32, (chunk, chunk), 0)
    si = lax.broadcasted_iota(jnp.int32, (chunk, chunk), 1)
    causal = si <= li
    tri = causal.astype(BF16)
    acum = sum(jnp.dot(tri, piece, preferred_element_type=F32) for piece in _split3(adt))
    acum_t = acum.T
    acumt_ref[...] = acum_t
    dt_t = dt.T
    dtt_ref[...] = dt_t
    wt_ref[...] = dt_t * jnp.exp(acum_t[:, chunk - 1:chunk] - acum_t)
    acum_sp_ref[...] = sum(jnp.dot(piece, spread_ref[...], preferred_element_type=F32)
                           for piece in _split3(acum))

    lane_lo = lax.broadcasted_iota(jnp.int32, (chunk, LANES), 1) < SSD_HEAD_DIM
    lane_lo_row = lane_lo[0:1, :]

    def conv_silu(col, width):
        cols = pl.ds(col, width)
        w = convw_ref[:, cols]
        acc = convb_ref[:, cols]
        for tap in range(SSD_D_CONV):
            shift = SSD_D_CONV - 1 - tap
            acc = acc + w[tap:tap + 1, :] * cbuf[pl.ds(SUBLANES - shift, chunk), cols]
        return acc * _sigmoid(acc)

    def group_body(g, _):
        xcol = pl.multiple_of(g * gw, gw)
        ncol = pl.multiple_of(g * SSD_D_STATE, SSD_D_STATE)
        xs = conv_silu(xcol, gw)
        bg = conv_silu(d_inner + ncol, SSD_D_STATE)
        cg = conv_silu(d_inner + nbc + ncol, SSD_D_STATE)
        scores = lax.dot_general(cg.astype(BF16), bg.astype(BF16), (((1,), (1,)), ((), ())),
                                 preferred_element_type=F32)
        bg_t = bg.T
        hrow = pl.multiple_of(g * SUBLANES, SUBLANES)
        acum_g = acum_sp_ref[:, pl.ds(ncol, LANES)]
        acumt_g = acumt_ref[pl.ds(hrow, heads_per_group), :]
        dtt_g = dtt_ref[pl.ds(hrow, heads_per_group), :]
        wt_g = wt_ref[pl.ds(hrow, heads_per_group), :]
        st = state_ref[g]

        pieces = []
        for pr in range(heads_per_group // 2):
            lhs_diag, lhs_off, lhs_state, cd_rows = [], [], [], []
            for r in (2 * pr, 2 * pr + 1):
                colb = jnp.broadcast_to(acum_g[:, r:r + 1], (chunk, chunk))
                seg = colb - acumt_g[r:r + 1, :]
                decay = jnp.where(causal, jnp.exp(seg), 0.0)
                lhs_diag.append((decay * scores * dtt_g[r:r + 1, :]).astype(BF16))
                lhs_off.append((cg * jnp.exp(colb)).astype(BF16))
                lhs_state.append((bg_t * wt_g[r:r + 1, :]).astype(BF16))
                cd_rows.append(jnp.exp(colb[chunk - 1:chunk, :]))
            psl = slice(pr * LANES, (pr + 1) * LANES)
            xp = xs[:, psl]
            x_bd = jnp.concatenate([jnp.where(lane_lo, xp, 0.0).astype(BF16),
                                    jnp.where(lane_lo, 0.0, xp).astype(BF16)], axis=0)
            sp = st[:, psl]
            s_bd = jnp.concatenate([jnp.where(lane_lo, sp, 0.0).astype(BF16),
                                    jnp.where(lane_lo, 0.0, sp).astype(BF16)], axis=0)
            y_pair = (jnp.dot(jnp.concatenate(lhs_diag, axis=1), x_bd, preferred_element_type=F32)
                      + jnp.dot(jnp.concatenate(lhs_off, axis=1), s_bd, preferred_element_type=F32))
            contrib = jnp.dot(jnp.concatenate(lhs_state, axis=1), x_bd, preferred_element_type=F32)
            cd = jnp.where(lane_lo_row, cd_rows[0], cd_rows[1])
            state_ref[g, :, psl] = sp * cd + contrib
            pieces.append(y_pair)

        xsl = pl.ds(xcol, gw)
        y = jnp.concatenate(pieces, axis=1) + dexp_ref[:, xsl] * xs
        zt = z_ref[:, xsl]
        y = y * (zt * _sigmoid(zt))
        o_ref[:, xsl] = _rms_normalize(y, gnw_ref[:, xsl], GATED_NORM_EPS).astype(o_ref.dtype)
        return 0

    lax.fori_loop(0, SSD_N_GROUPS, group_body, 0)

    cbuf[0:SUBLANES, :] = cbuf[chunk:chunk + SUBLANES, :]


def _ssd_mixer(proj, dt_raw, conv_w, conv_b, dt_bias_pad, a_log_pad, d_exp, gnorm_w, batch, seq,
               d_inner):
    t = proj.shape[0]
    chunk = SSD_CHUNK
    nc = seq // chunk
    nbc = SSD_N_GROUPS * SSD_D_STATE
    conv_dim = d_inner + 2 * nbc
    gw = d_inner // SSD_N_GROUPS
    heads_per_group = gw // SSD_HEAD_DIM
    assert heads_per_group == SUBLANES and d_inner % nbc == 0
    hh = jnp.arange(LANES)[:, None]
    cc = jnp.arange(SSD_N_GROUPS * LANES)[None, :]
    spread = ((cc % LANES < heads_per_group)
              & (hh == (cc // LANES) * heads_per_group + cc % LANES)).astype(BF16)
    row = lambda b, c: b * nc + c
    blocks = [2 * _nbytes((chunk, d_inner), F32), 2 * _nbytes((chunk, nbc), F32),
              _nbytes((chunk, LANES), F32), _nbytes(spread.shape, BF16),
              _nbytes((8, conv_dim), F32) * 2, _nbytes((chunk, d_inner), BF16)]
    scratch = (_nbytes((chunk + SUBLANES, conv_dim), F32) + _nbytes((SSD_N_GROUPS, SSD_D_STATE, gw), F32)
               + _nbytes((chunk, SSD_N_GROUPS * LANES), F32) + 3 * _nbytes((LANES, chunk), F32))
    full = lambda shape: pl.BlockSpec(shape, lambda b, c: (0, 0))
    return pl.pallas_call(
        functools.partial(_ssd_kernel, d_inner=d_inner),
        grid=(batch, nc),
        in_specs=[
            pl.BlockSpec((chunk, d_inner), lambda b, c: (row(b, c), 0)),
            pl.BlockSpec((chunk, d_inner), lambda b, c: (row(b, c), 1)),
            pl.BlockSpec((chunk, nbc), lambda b, c: (row(b, c), 2 * d_inner // nbc)),
            pl.BlockSpec((chunk, nbc), lambda b, c: (row(b, c), 2 * d_inner // nbc + 1)),
            pl.BlockSpec((chunk, LANES), lambda b, c: (row(b, c), 0)),
            full(spread.shape),
            full((SSD_D_CONV, conv_dim)),
            full((1, conv_dim)),
            full((1, LANES)),
            full((1, LANES)),
            full((1, d_inner)),
            full((1, d_inner)),
        ],
        out_specs=pl.BlockSpec((chunk, d_inner), lambda b, c: (row(b, c), 0)),
        out_shape=jax.ShapeDtypeStruct((t, d_inner), BF16),
        scratch_shapes=[
            pltpu.VMEM((chunk + SUBLANES, conv_dim), F32),
            pltpu.VMEM((SSD_N_GROUPS, SSD_D_STATE, gw), F32),
            pltpu.VMEM((chunk, SSD_N_GROUPS * LANES), F32),
            pltpu.VMEM((LANES, chunk), F32),
            pltpu.VMEM((LANES, chunk), F32),
            pltpu.VMEM((LANES, chunk), F32),
        ],
        compiler_params=pltpu.CompilerParams(
            dimension_semantics=("parallel", "arbitrary"),
            vmem_limit_bytes=_vmem_limit(blocks, scratch)),
        name="ssd_mixer",
    )(proj, proj, proj, proj, dt_raw, spread, conv_w, conv_b.reshape(1, conv_dim),
      dt_bias_pad, a_log_pad, d_exp, gnorm_w.reshape(1, d_inner))


def _pad_lanes(v):
    return jnp.pad(v.astype(F32), (0, LANES - v.shape[0])).reshape(1, LANES)


def _ssd_branch(h, nw, in_w, conv_w, conv_b, dt_bias, a_log, d_skip, gnorm_w, out_w, batch, seq):
    d_inner = out_w.shape[0]
    nbc = SSD_N_GROUPS * SSD_D_STATE
    main = d_inner + d_inner + 2 * nbc
    n_heads = in_w.shape[1] - main
    w_main = in_w[:, :main].astype(BF16)
    w_dt = jnp.pad(in_w[:, main:], ((0, 0), (0, LANES - n_heads))).astype(BF16)
    proj = _norm_matmul(h, nw, w_main, F32)
    dt_raw = _norm_matmul(h, nw, w_dt, F32)
    d_exp = jnp.repeat(d_skip.astype(F32), SSD_HEAD_DIM).reshape(1, d_inner)
    y = _ssd_mixer(proj, dt_raw, conv_w, conv_b, _pad_lanes(dt_bias), _pad_lanes(a_log), d_exp,
                   gnorm_w, batch, seq, d_inner)
    return _matmul_residual(y, out_w.astype(BF16), h)


def _sb_branch(h, nw, in_w, qn_w, kn_w, out_w, batch, seq):
    qkvg = _sb_in_proj(h, nw, in_w.astype(BF16), qn_w, kn_w)
    o = _sb_attention(qkvg, batch, seq)
    return _matmul_residual(o, out_w.astype(BF16), h)


def kernel(x, p, norm_w, ssd_in_w, ssd_conv_w, ssd_conv_b, ssd_dt_bias, ssd_a_log, ssd_d, ssd_gnorm_w, ssd_out_w, sb_in_w, sb_qn_w, sb_kn_w, sb_out_w, ple_norm_w, ple_gate_w, ple_proj_w):
    batch, seq, d_model = x.shape
    depth = p.shape[0]
    h = x.reshape(batch * seq, d_model)
    for i in range(depth):
        j = i // 2
        if i % 2 == 0:
            h = _ssd_branch(h, norm_w[i], ssd_in_w[j], ssd_conv_w[j], ssd_conv_b[j], ssd_dt_bias[j],
                            ssd_a_log[j], ssd_d[j], ssd_gnorm_w[j], ssd_out_w[j], batch, seq)
        else:
            h = _sb_branch(h, norm_w[i], sb_in_w[j], sb_qn_w[j], sb_kn_w[j], sb_out_w[j], batch, seq)
        h = _ple(h, ple_norm_w[i], ple_gate_w[i].astype(BF16), p[i].reshape(batch * seq, -1),
                 ple_proj_w[i].astype(BF16))
    return h.reshape(batch, seq, d_model)
```

```python
import functools
import math

import jax
import jax.numpy as jnp
from jax import lax
from jax.experimental import pallas as pl
from jax.experimental.pallas import tpu as pltpu

F32 = jnp.float32
BF16 = jnp.bfloat16

NORM_EPS = 1e-6
GATED_NORM_EPS = 1e-5
LOG2E = 1.4426950408889634
SIGN_BIT = 0x80000000

SSD_HEAD_DIM = 64
SSD_N_GROUPS = 8
SSD_D_STATE = 128
SSD_D_CONV = 4
SSD_CHUNK = 128
SB_HEAD_DIM = 128

LANES = 128
SUBLANES = 8
V7X_VMEM_BYTES = 64 * 1024 * 1024
COMPILER_SCRATCH_BYTES = 12 * 1024 * 1024


def _vmem_limit(block_bytes, scratch_bytes=0):
    need = 2 * sum(block_bytes) + scratch_bytes + COMPILER_SCRATCH_BYTES
    return int(min(need, V7X_VMEM_BYTES - 8 * 1024 * 1024))


def _nbytes(shape, dtype):
    return math.prod(shape) * jnp.dtype(dtype).itemsize


def _pick_tile(n, pref):
    t = min(n, pref)
    assert n % t == 0, (n, t)
    return t


def _sigmoid(x):
    return 1.0 / (1.0 + jnp.exp(-x))


def _softplus(x):
    return jnp.maximum(x, 0.0) + jnp.log(1.0 + jnp.exp(-jnp.abs(x)))


def _rms_normalize(x, w, eps):
    ms = jnp.mean(x * x, axis=-1, keepdims=True)
    return x * lax.rsqrt(ms + eps) * w


def _norm_matmul_kernel(x_ref, nw_ref, w_ref, o_ref, xn_ref):
    @pl.when(pl.program_id(1) == 0)
    def _():
        xn_ref[...] = _rms_normalize(x_ref[...], nw_ref[...], NORM_EPS).astype(xn_ref.dtype)

    o_ref[...] = jnp.dot(xn_ref[...], w_ref[...], preferred_element_type=F32).astype(o_ref.dtype)


def _norm_matmul(x, nw, w, out_dtype, tm_pref=1024, tn_pref=512):
    t, d = x.shape
    n = w.shape[1]
    tm, tn = _pick_tile(t, tm_pref), _pick_tile(n, tn_pref)
    blocks = [_nbytes((tm, d), F32), _nbytes((d, tn), BF16), _nbytes((tm, tn), out_dtype)]
    return pl.pallas_call(
        _norm_matmul_kernel,
        grid=(t // tm, n // tn),
        in_specs=[
            pl.BlockSpec((tm, d), lambda i, j: (i, 0)),
            pl.BlockSpec((1, d), lambda i, j: (0, 0)),
            pl.BlockSpec((d, tn), lambda i, j: (0, j)),
        ],
        out_specs=pl.BlockSpec((tm, tn), lambda i, j: (i, j)),
        out_shape=jax.ShapeDtypeStruct((t, n), out_dtype),
        scratch_shapes=[pltpu.VMEM((tm, d), BF16)],
        compiler_params=pltpu.CompilerParams(
            dimension_semantics=("parallel", "arbitrary"),
            vmem_limit_bytes=_vmem_limit(blocks, _nbytes((tm, d), BF16))),
        name="norm_matmul",
    )(x, nw.reshape(1, d), w)


def _sb_in_proj_kernel(x_ref, nw_ref, w_ref, qn_ref, kn_ref, o_ref, xn_ref, *, tiles_per_section):
    j = pl.program_id(1)

    @pl.when(j == 0)
    def _():
        xn_ref[...] = _rms_normalize(x_ref[...], nw_ref[...], NORM_EPS).astype(xn_ref.dtype)

    acc = jnp.dot(xn_ref[...], w_ref[...], preferred_element_type=F32)
    section = j // tiles_per_section
    heads_per_tile = acc.shape[1] // SB_HEAD_DIM

    def head_norm(hw_ref, scale):
        for hh in range(heads_per_tile):
            sl = slice(hh * SB_HEAD_DIM, (hh + 1) * SB_HEAD_DIM)
            y = _rms_normalize(acc[:, sl], hw_ref[...], NORM_EPS)
            o_ref[:, sl] = (y * scale).astype(o_ref.dtype)

    @pl.when(section == 0)
    def _():
        head_norm(qn_ref, LOG2E / math.sqrt(SB_HEAD_DIM))

    @pl.when(section == 1)
    def _():
        head_norm(kn_ref, 1.0)

    @pl.when(section == 2)
    def _():
        o_ref[...] = acc.astype(o_ref.dtype)

    @pl.when(section == 3)
    def _():
        o_ref[...] = (acc * _sigmoid(acc)).astype(o_ref.dtype)


def _sb_in_proj(x, nw, w, qn_w, kn_w, tm_pref=1024, tn_pref=512):
    t, d = x.shape
    n = w.shape[1]
    width = n // 4
    tm, tn = _pick_tile(t, tm_pref), _pick_tile(width, tn_pref)
    blocks = [_nbytes((tm, d), F32), _nbytes((d, tn), BF16), _nbytes((tm, tn), BF16)]
    return pl.pallas_call(
        functools.partial(_sb_in_proj_kernel, tiles_per_section=width // tn),
        grid=(t // tm, n // tn),
        in_specs=[
            pl.BlockSpec((tm, d), lambda i, j: (i, 0)),
            pl.BlockSpec((1, d), lambda i, j: (0, 0)),
            pl.BlockSpec((d, tn), lambda i, j: (0, j)),
            pl.BlockSpec((1, SB_HEAD_DIM), lambda i, j: (0, 0)),
            pl.BlockSpec((1, SB_HEAD_DIM), lambda i, j: (0, 0)),
        ],
        out_specs=pl.BlockSpec((tm, tn), lambda i, j: (i, j)),
        out_shape=jax.ShapeDtypeStruct((t, n), BF16),
        scratch_shapes=[pltpu.VMEM((tm, d), BF16)],
        compiler_params=pltpu.CompilerParams(
            dimension_semantics=("parallel", "arbitrary"),
            vmem_limit_bytes=_vmem_limit(blocks, _nbytes((tm, d), BF16))),
        name="sb_in_proj",
    )(x, nw.reshape(1, d), w, qn_w.reshape(1, SB_HEAD_DIM), kn_w.reshape(1, SB_HEAD_DIM))


def _matmul_residual_kernel(a_ref, w_ref, h_ref, o_ref):
    o_ref[...] = h_ref[...] + jnp.dot(a_ref[...], w_ref[...], preferred_element_type=F32)


def _matmul_residual(a, w, h, tm_pref=1024, tn_pref=512):
    t, k = a.shape
    n = w.shape[1]
    tm, tn = _pick_tile(t, tm_pref), _pick_tile(n, tn_pref)
    blocks = [_nbytes((tm, k), BF16), _nbytes((k, tn), BF16), 2 * _nbytes((tm, tn), F32)]
    return pl.pallas_call(
        _matmul_residual_kernel,
        grid=(t // tm, n // tn),
        in_specs=[
            pl.BlockSpec((tm, k), lambda i, j: (i, 0)),
            pl.BlockSpec((k, tn), lambda i, j: (0, j)),
            pl.BlockSpec((tm, tn), lambda i, j: (i, j)),
        ],
        out_specs=pl.BlockSpec((tm, tn), lambda i, j: (i, j)),
        out_shape=jax.ShapeDtypeStruct((t, n), F32),
        compiler_params=pltpu.CompilerParams(
            dimension_semantics=("parallel", "arbitrary"),
            vmem_limit_bytes=_vmem_limit(blocks)),
        name="matmul_residual",
    )(a, w, h)


def _ple_kernel(h_ref, nw_ref, gw_ref, p_ref, pw_ref, hres_ref, o_ref, xn_ref):
    @pl.when(pl.program_id(1) == 0)
    def _():
        xn_ref[...] = _rms_normalize(h_ref[...], nw_ref[...], NORM_EPS).astype(xn_ref.dtype)

    gate = _sigmoid(jnp.dot(xn_ref[...], gw_ref[...], preferred_element_type=F32))
    emb = jnp.dot(p_ref[...].astype(BF16), pw_ref[...], preferred_element_type=F32)
    o_ref[...] = hres_ref[...] + emb * gate


def _ple(h, nw, gw, p, pw, tm_pref=1024, tn_pref=512):
    t, d = h.shape
    pd = p.shape[1]
    tm, tn = _pick_tile(t, tm_pref), _pick_tile(d, tn_pref)
    blocks = [_nbytes((tm, d), F32), _nbytes((d, tn), BF16), _nbytes((tm, pd), F32),
              _nbytes((pd, tn), BF16), 2 * _nbytes((tm, tn), F32)]
    return pl.pallas_call(
        _ple_kernel,
        grid=(t // tm, d // tn),
        in_specs=[
            pl.BlockSpec((tm, d), lambda i, j: (i, 0)),
            pl.BlockSpec((1, d), lambda i, j: (0, 0)),
            pl.BlockSpec((d, tn), lambda i, j: (0, j)),
            pl.BlockSpec((tm, pd), lambda i, j: (i, 0)),
            pl.BlockSpec((pd, tn), lambda i, j: (0, j)),
            pl.BlockSpec((tm, tn), lambda i, j: (i, j)),
        ],
        out_specs=pl.BlockSpec((tm, tn), lambda i, j: (i, j)),
        out_shape=jax.ShapeDtypeStruct((t, d), F32),
        scratch_shapes=[pltpu.VMEM((tm, d), BF16)],
        compiler_params=pltpu.CompilerParams(
            dimension_semantics=("parallel", "arbitrary"),
            vmem_limit_bytes=_vmem_limit(blocks, _nbytes((tm, d), BF16))),
        name="ple",
    )(h, nw.reshape(1, d), gw, p, pw, h)


def _sb_attn_kernel(q_ref, k_ref, v_ref, g_ref, o_ref, *, blk, heads_per_step, strips):
    i = pl.program_id(2)
    hd = SB_HEAD_DIM
    row = lax.broadcasted_iota(jnp.int32, (blk, blk), 0)
    col = lax.broadcasted_iota(jnp.int32, (blk, blk), 1)
    strict = col < row
    later = (row > col).astype(BF16)
    later2 = jnp.concatenate([later, later], axis=0)

    rows = blk // strips
    units = [(slice(hh * hd, (hh + 1) * hd), slice(st * rows, (st + 1) * rows))
             for hh in range(heads_per_step) for st in range(strips)]

    def block(j, state, diagonal):
        start = pl.multiple_of(j * blk, blk)
        zs = [lax.dot_general(q_ref[rs, hs], k_ref[pl.ds(start, blk), hs], (((1,), (1,)), ((), ())),
                              preferred_element_type=F32) for hs, rs in units]
        logits, carries = [], []
        for u, z in enumerate(zs):
            neg_abs = lax.bitcast_convert_type(
                lax.bitcast_convert_type(z, jnp.uint32) | jnp.uint32(SIGN_BIT), F32)
            log_beta = jnp.minimum(z, 0.0) - jnp.log(1.0 + jnp.exp2(neg_abs)) * LOG2E
            l1m = log_beta - z
            if diagonal:
                l1m = jnp.where(strict[units[u][1], :], l1m, 0.0)
            hi = l1m.astype(BF16)
            lo = (l1m - hi.astype(F32)).astype(BF16)
            rest = jnp.dot(jnp.concatenate([hi, lo], axis=1), later2, preferred_element_type=F32)
            logits.append(log_beta + rest + state[u][0])
            carries.append(state[u][0] + jnp.sum(l1m, axis=-1, keepdims=True))
        out = []
        for u, logit in enumerate(logits):
            att = jnp.exp2(logit)
            if diagonal:
                att = jnp.where(strict[units[u][1], :], att, 0.0)
            acc = state[u][1] + jnp.dot(att.astype(BF16), v_ref[pl.ds(start, blk), units[u][0]],
                                        preferred_element_type=F32)
            out.append((carries[u], acc))
        return tuple(out)

    init = tuple((jnp.zeros((rows, 1), F32), jnp.zeros((rows, hd), F32)) for _ in units)
    state = block(i, init, True)
    state = lax.fori_loop(0, i, lambda t, s: block(i - 1 - t, s, False), state)
    for u, (hs, rs) in enumerate(units):
        o_ref[rs, hs] = (state[u][1] * g_ref[rs, hs].astype(F32)).astype(o_ref.dtype)


def _sb_attention(qkvg, batch, seq, blk_pref=256, heads_per_step=8, strips=1):
    t, n = qkvg.shape
    heads = n // (4 * SB_HEAD_DIM)
    blk = _pick_tile(seq, blk_pref)
    nq = seq // blk
    assert heads % heads_per_step == 0
    hsteps = heads // heads_per_step
    w = heads_per_step * SB_HEAD_DIM
    blocks = [_nbytes((blk, w), BF16), 2 * _nbytes((seq, w), BF16), 2 * _nbytes((blk, w), BF16)]
    return pl.pallas_call(
        functools.partial(_sb_attn_kernel, blk=blk, heads_per_step=heads_per_step, strips=strips),
        grid=(batch, hsteps, nq),
        in_specs=[
            pl.BlockSpec((blk, w), lambda b, h, i: (b * nq + i, h)),
            pl.BlockSpec((seq, w), lambda b, h, i: (b, hsteps + h)),
            pl.BlockSpec((seq, w), lambda b, h, i: (b, 2 * hsteps + h)),
            pl.BlockSpec((blk, w), lambda b, h, i: (b * nq + i, 3 * hsteps + h)),
        ],
        out_specs=pl.BlockSpec((blk, w), lambda b, h, i: (b * nq + i, h)),
        out_shape=jax.ShapeDtypeStruct((t, heads * SB_HEAD_DIM), BF16),
        compiler_params=pltpu.CompilerParams(
            dimension_semantics=("parallel", "parallel", "arbitrary"),
            vmem_limit_bytes=_vmem_limit(blocks)),
        name="sb_attention",
    )(qkvg, qkvg, qkvg, qkvg)


def _split3(v):
    hi = v.astype(BF16)
    r = v - hi.astype(F32)
    mid = r.astype(BF16)
    lo = (r - mid.astype(F32)).astype(BF16)
    return hi, mid, lo


def _ssd_kernel(z_ref, x_ref, b_ref, c_ref, dt_ref, spread_ref, convw_ref, convb_ref, dtb_ref,
                alog_ref, dexp_ref, gnw_ref, o_ref,
                cbuf, state_ref, acum_sp_ref, ea_sp_ref, rowb_ref, wt_ref, *, d_inner):
    chunk = SSD_CHUNK
    heads_per_group = d_inner // SSD_N_GROUPS // SSD_HEAD_DIM
    gw = d_inner // SSD_N_GROUPS
    nbc = SSD_N_GROUPS * SSD_D_STATE
    first = pl.program_id(1) == 0

    @pl.when(first)
    def _():
        cbuf[0:SUBLANES, :] = jnp.zeros((SUBLANES, cbuf.shape[1]), F32)
        state_ref[...] = jnp.zeros(state_ref.shape, F32)

    cbuf[SUBLANES:SUBLANES + chunk, 0:d_inner] = x_ref[...].astype(F32)
    cbuf[SUBLANES:SUBLANES + chunk, d_inner:d_inner + nbc] = b_ref[...].astype(F32)
    cbuf[SUBLANES:SUBLANES + chunk, d_inner + nbc:d_inner + 2 * nbc] = c_ref[...].astype(F32)

    dt = _softplus(dt_ref[...] + dtb_ref[...])
    adt = dt * (-LOG2E * jnp.exp(alog_ref[...]))
    li = lax.broadcasted_iota(jnp.int32, (chunk, chunk), 0)
    si = lax.broadcasted_iota(jnp.int32, (chunk, chunk), 1)
    causal = si <= li
    tri = causal.astype(BF16)
    acum = sum(jnp.dot(tri, piece, preferred_element_type=F32) for piece in _split3(adt))
    acum_t = acum.T
    dt_t = dt.T
    rowb_ref[...] = acum_t - jnp.log(dt_t) * LOG2E
    wt_ref[...] = dt_t * jnp.exp2(acum_t[:, chunk - 1:chunk] - acum_t)
    acum_sp_ref[...] = sum(jnp.dot(piece, spread_ref[...], preferred_element_type=F32)
                           for piece in _split3(acum))
    ea_sp_ref[...] = sum(jnp.dot(piece, spread_ref[...], preferred_element_type=F32)
                         for piece in _split3(jnp.exp2(acum)))

    lane_lo = lax.broadcasted_iota(jnp.int32, (chunk, LANES), 1) < SSD_HEAD_DIM

    def conv_silu(col, width):
        cols = pl.ds(col, width)
        w = convw_ref[:, cols]
        acc = convb_ref[:, cols]
        for tap in range(SSD_D_CONV):
            shift = SSD_D_CONV - 1 - tap
            acc = acc + w[tap:tap + 1, :] * cbuf[pl.ds(SUBLANES - shift, chunk), cols]
        return acc * _sigmoid(acc)

    def group_body(g, _):
        xcol = pl.multiple_of(g * gw, gw)
        ncol = pl.multiple_of(g * SSD_D_STATE, SSD_D_STATE)
        xs = conv_silu(xcol, gw)
        bg = conv_silu(d_inner + ncol, SSD_D_STATE)
        cg = conv_silu(d_inner + nbc + ncol, SSD_D_STATE)
        scores = lax.dot_general(cg.astype(BF16), bg.astype(BF16), (((1,), (1,)), ((), ())),
                                 preferred_element_type=F32)
        bg_t = bg.T
        hrow = pl.multiple_of(g * SUBLANES, SUBLANES)
        acum_g = acum_sp_ref[:, pl.ds(ncol, LANES)]
        ea_g = ea_sp_ref[:, pl.ds(ncol, LANES)]
        rowb_g = rowb_ref[pl.ds(hrow, heads_per_group), :]
        wt_g = wt_ref[pl.ds(hrow, heads_per_group), :]
        st = state_ref[g]
        y_off = jnp.dot(cg.astype(BF16), st.astype(BF16), preferred_element_type=F32)

        pieces = []
        for pr in range(heads_per_group // 2):
            lhs_diag, lhs_state, ea_cols = [], [], []
            for r in (2 * pr, 2 * pr + 1):
                colb = jnp.broadcast_to(acum_g[:, r:r + 1], (chunk, chunk))
                decay_dt = jnp.where(causal, jnp.exp2(colb - rowb_g[r:r + 1, :]), 0.0)
                lhs_diag.append((decay_dt * scores).astype(BF16))
                lhs_state.append((bg_t * wt_g[r:r + 1, :]).astype(BF16))
                ea_cols.append(jnp.broadcast_to(ea_g[:, r:r + 1], (chunk, LANES)))
            psl = slice(pr * LANES, (pr + 1) * LANES)
            xp = xs[:, psl]
            x_bd = jnp.concatenate([jnp.where(lane_lo, xp, 0.0).astype(BF16),
                                    jnp.where(lane_lo, 0.0, xp).astype(BF16)], axis=0)
            ea_pair = jnp.where(lane_lo, ea_cols[0], ea_cols[1])
            y_pair = (jnp.dot(jnp.concatenate(lhs_diag, axis=1), x_bd, preferred_element_type=F32)
                      + ea_pair * y_off[:, psl])
            contrib = jnp.dot(jnp.concatenate(lhs_state, axis=1), x_bd, preferred_element_type=F32)
            state_ref[g, :, psl] = st[:, psl] * ea_pair[chunk - 1:chunk, :] + contrib
            pieces.append(y_pair)

        xsl = pl.ds(xcol, gw)
        y = jnp.concatenate(pieces, axis=1) + dexp_ref[:, xsl] * xs
        zt = z_ref[:, xsl].astype(F32)
        y = y * (zt * _sigmoid(zt))
        o_ref[:, xsl] = _rms_normalize(y, gnw_ref[:, xsl], GATED_NORM_EPS).astype(o_ref.dtype)
        return 0

    lax.fori_loop(0, SSD_N_GROUPS, group_body, 0)

    cbuf[0:SUBLANES, :] = cbuf[chunk:chunk + SUBLANES, :]


def _ssd_mixer(proj, dt_raw, conv_w, conv_b, dt_bias_pad, a_log_pad, d_exp, gnorm_w, batch, seq,
               d_inner):
    t = proj.shape[0]
    chunk = SSD_CHUNK
    nc = seq // chunk
    nbc = SSD_N_GROUPS * SSD_D_STATE
    conv_dim = d_inner + 2 * nbc
    gw = d_inner // SSD_N_GROUPS
    heads_per_group = gw // SSD_HEAD_DIM
    assert heads_per_group == SUBLANES and d_inner % nbc == 0
    hh = jnp.arange(LANES)[:, None]
    cc = jnp.arange(SSD_N_GROUPS * LANES)[None, :]
    spread = ((cc % LANES < heads_per_group)
              & (hh == (cc // LANES) * heads_per_group + cc % LANES)).astype(BF16)
    row = lambda b, c: b * nc + c
    blocks = [2 * _nbytes((chunk, d_inner), proj.dtype), 2 * _nbytes((chunk, nbc), proj.dtype),
              _nbytes((chunk, LANES), F32), _nbytes(spread.shape, BF16),
              _nbytes((8, conv_dim), F32) * 2, _nbytes((chunk, d_inner), BF16)]
    scratch = (_nbytes((chunk + SUBLANES, conv_dim), F32) + _nbytes((SSD_N_GROUPS, SSD_D_STATE, gw), F32)
               + 2 * _nbytes((chunk, SSD_N_GROUPS * LANES), F32) + 2 * _nbytes((LANES, chunk), F32))
    full = lambda shape: pl.BlockSpec(shape, lambda b, c: (0, 0))
    return pl.pallas_call(
        functools.partial(_ssd_kernel, d_inner=d_inner),
        grid=(batch, nc),
        in_specs=[
            pl.BlockSpec((chunk, d_inner), lambda b, c: (row(b, c), 0)),
            pl.BlockSpec((chunk, d_inner), lambda b, c: (row(b, c), 1)),
            pl.BlockSpec((chunk, nbc), lambda b, c: (row(b, c), 2 * d_inner // nbc)),
            pl.BlockSpec((chunk, nbc), lambda b, c: (row(b, c), 2 * d_inner // nbc + 1)),
            pl.BlockSpec((chunk, LANES), lambda b, c: (row(b, c), 0)),
            full(spread.shape),
            full((SSD_D_CONV, conv_dim)),
            full((1, conv_dim)),
            full((1, LANES)),
            full((1, LANES)),
            full((1, d_inner)),
            full((1, d_inner)),
        ],
        out_specs=pl.BlockSpec((chunk, d_inner), lambda b, c: (row(b, c), 0)),
        out_shape=jax.ShapeDtypeStruct((t, d_inner), BF16),
        scratch_shapes=[
            pltpu.VMEM((chunk + SUBLANES, conv_dim), F32),
            pltpu.VMEM((SSD_N_GROUPS, SSD_D_STATE, gw), F32),
            pltpu.VMEM((chunk, SSD_N_GROUPS * LANES), F32),
            pltpu.VMEM((chunk, SSD_N_GROUPS * LANES), F32),
            pltpu.VMEM((LANES, chunk), F32),
            pltpu.VMEM((LANES, chunk), F32),
        ],
        compiler_params=pltpu.CompilerParams(
            dimension_semantics=("parallel", "arbitrary"),
            vmem_limit_bytes=_vmem_limit(blocks, scratch)),
        name="ssd_mixer",
    )(proj, proj, proj, proj, dt_raw, spread, conv_w, conv_b.reshape(1, conv_dim),
      dt_bias_pad, a_log_pad, d_exp, gnorm_w.reshape(1, d_inner))


def _pad_lanes(v):
    return jnp.pad(v.astype(F32), (0, LANES - v.shape[0])).reshape(1, LANES)


def _ssd_branch(h, nw, in_w, conv_w, conv_b, dt_bias, a_log, d_skip, gnorm_w, out_w, batch, seq):
    d_inner = out_w.shape[0]
    nbc = SSD_N_GROUPS * SSD_D_STATE
    main = d_inner + d_inner + 2 * nbc
    n_heads = in_w.shape[1] - main
    w_main = in_w[:, :main].astype(BF16)
    w_dt = jnp.pad(in_w[:, main:], ((0, 0), (0, LANES - n_heads))).astype(BF16)
    proj = _norm_matmul(h, nw, w_main, BF16)
    dt_raw = _norm_matmul(h, nw, w_dt, F32)
    d_exp = jnp.repeat(d_skip.astype(F32), SSD_HEAD_DIM).reshape(1, d_inner)
    y = _ssd_mixer(proj, dt_raw, conv_w, conv_b, _pad_lanes(dt_bias), _pad_lanes(a_log), d_exp,
                   gnorm_w, batch, seq, d_inner)
    return _matmul_residual(y, out_w.astype(BF16), h)


def _sb_branch(h, nw, in_w, qn_w, kn_w, out_w, batch, seq):
    qkvg = _sb_in_proj(h, nw, in_w.astype(BF16), qn_w, kn_w)
    o = _sb_attention(qkvg, batch, seq)
    return _matmul_residual(o, out_w.astype(BF16), h)


def kernel(x, p, norm_w, ssd_in_w, ssd_conv_w, ssd_conv_b, ssd_dt_bias, ssd_a_log, ssd_d, ssd_gnorm_w, ssd_out_w, sb_in_w, sb_qn_w, sb_kn_w, sb_out_w, ple_norm_w, ple_gate_w, ple_proj_w):
    batch, seq, d_model = x.shape
    depth = p.shape[0]
    h = x.reshape(batch * seq, d_model)
    for i in range(depth):
        j = i // 2
        if i % 2 == 0:
            h = _ssd_branch(h, norm_w[i], ssd_in_w[j], ssd_conv_w[j], ssd_conv_b[j], ssd_dt_bias[j],
                            ssd_a_log[j], ssd_d[j], ssd_gnorm_w[j], ssd_out_w[j], batch, seq)
        else:
            h = _sb_branch(h, norm_w[i], sb_in_w[j], sb_qn_w[j], sb_kn_w[j], sb_out_w[j], batch, seq)
        h = _ple(h, ple_norm_w[i], ple_gate_w[i].astype(BF16), p[i].reshape(batch * seq, -1),
                 ple_proj_w[i].astype(BF16))
    return h.reshape(batch, seq, d_model)
```

```python
import functools
import math

import jax
import jax.numpy as jnp
from jax import lax
from jax.experimental import pallas as pl
from jax.experimental.pallas import tpu as pltpu

F32 = jnp.float32
BF16 = jnp.bfloat16

NORM_EPS = 1e-6
GATED_NORM_EPS = 1e-5
LOG2E = 1.4426950408889634
SIGN_BIT = 0x80000000

SSD_HEAD_DIM = 64
SSD_N_GROUPS = 8
SSD_D_STATE = 128
SSD_D_CONV = 4
SSD_CHUNK = 128
SB_HEAD_DIM = 128

LANES = 128
SUBLANES = 8
V7X_VMEM_BYTES = 64 * 1024 * 1024
COMPILER_SCRATCH_BYTES = 12 * 1024 * 1024


def _vmem_limit(block_bytes, scratch_bytes=0):
    need = 2 * sum(block_bytes) + scratch_bytes + COMPILER_SCRATCH_BYTES
    return int(min(need, V7X_VMEM_BYTES - 8 * 1024 * 1024))


def _nbytes(shape, dtype):
    return math.prod(shape) * jnp.dtype(dtype).itemsize


def _pick_tile(n, pref):
    t = min(n, pref)
    assert n % t == 0, (n, t)
    return t


def _sigmoid(x):
    return 1.0 / (1.0 + jnp.exp(-x))


def _softplus(x):
    return jnp.maximum(x, 0.0) + jnp.log(1.0 + jnp.exp(-jnp.abs(x)))


def _rms_normalize(x, w, eps):
    ms = jnp.mean(x * x, axis=-1, keepdims=True)
    return x * lax.rsqrt(ms + eps) * w


STRIP_ROWS = 256


def _row_strips(rows):
    step = min(rows, STRIP_ROWS)
    return [slice(r, r + step) for r in range(0, rows, step)]


def _norm_matmul_kernel(x_ref, nw_ref, w_ref, o_ref, xn_ref):
    @pl.when(pl.program_id(1) == 0)
    def _():
        xn_ref[...] = _rms_normalize(x_ref[...], nw_ref[...], NORM_EPS).astype(xn_ref.dtype)

    o_ref[...] = jnp.dot(xn_ref[...], w_ref[...], preferred_element_type=F32).astype(o_ref.dtype)


def _norm_matmul(x, nw, w, layer, n, out_dtype, tm_pref=1024, tn_pref=1024):
    t, d = x.shape
    tm, tn = _pick_tile(t, tm_pref), _pick_tile(n, tn_pref)
    blocks = [_nbytes((tm, d), F32), _nbytes((d, tn), BF16), _nbytes((tm, tn), out_dtype)]
    return pl.pallas_call(
        _norm_matmul_kernel,
        grid=(t // tm, n // tn),
        in_specs=[
            pl.BlockSpec((tm, d), lambda i, j: (i, 0)),
            pl.BlockSpec((1, d), lambda i, j: (0, 0)),
            pl.BlockSpec((None, d, tn), lambda i, j: (layer, 0, j)),
        ],
        out_specs=pl.BlockSpec((tm, tn), lambda i, j: (i, j)),
        out_shape=jax.ShapeDtypeStruct((t, n), out_dtype),
        scratch_shapes=[pltpu.VMEM((tm, d), BF16)],
        compiler_params=pltpu.CompilerParams(
            dimension_semantics=("parallel", "arbitrary"),
            vmem_limit_bytes=_vmem_limit(blocks, _nbytes((tm, d), BF16))),
        name="norm_matmul",
    )(x, nw.reshape(1, d), w)


def _sb_in_proj_kernel(x_ref, nw_ref, w_ref, qn_ref, kn_ref, o_ref, xn_ref, *, tiles_per_section):
    j = pl.program_id(1)

    @pl.when(j == 0)
    def _():
        xn_ref[...] = _rms_normalize(x_ref[...], nw_ref[...], NORM_EPS).astype(xn_ref.dtype)

    section = j // tiles_per_section
    heads_per_tile = o_ref.shape[1] // SB_HEAD_DIM

    def project(epilogue):
        for rows in _row_strips(o_ref.shape[0]):
            acc = jnp.dot(xn_ref[rows, :], w_ref[...], preferred_element_type=F32)
            o_ref[rows, :] = epilogue(acc).astype(o_ref.dtype)

    def head_norm(hw_ref, scale):
        def epilogue(acc):
            heads = [_rms_normalize(acc[:, hh * SB_HEAD_DIM:(hh + 1) * SB_HEAD_DIM], hw_ref[...],
                                    NORM_EPS) * scale for hh in range(heads_per_tile)]
            return jnp.concatenate(heads, axis=1)
        return epilogue

    @pl.when(section == 0)
    def _():
        project(head_norm(qn_ref, LOG2E / math.sqrt(SB_HEAD_DIM)))

    @pl.when(section == 1)
    def _():
        project(head_norm(kn_ref, 1.0))

    @pl.when(section == 2)
    def _():
        project(lambda acc: acc)

    @pl.when(section == 3)
    def _():
        project(lambda acc: acc * _sigmoid(acc))


def _sb_in_proj(x, nw, w, layer, qn_w, kn_w, tm_pref=1024, tn_pref=1024):
    t, d = x.shape
    n = w.shape[2]
    width = n // 4
    tm, tn = _pick_tile(t, tm_pref), _pick_tile(width, tn_pref)
    blocks = [_nbytes((tm, d), F32), _nbytes((d, tn), BF16), _nbytes((tm, tn), BF16)]
    return pl.pallas_call(
        functools.partial(_sb_in_proj_kernel, tiles_per_section=width // tn),
        grid=(t // tm, n // tn),
        in_specs=[
            pl.BlockSpec((tm, d), lambda i, j: (i, 0)),
            pl.BlockSpec((1, d), lambda i, j: (0, 0)),
            pl.BlockSpec((None, d, tn), lambda i, j: (layer, 0, j)),
            pl.BlockSpec((1, SB_HEAD_DIM), lambda i, j: (0, 0)),
            pl.BlockSpec((1, SB_HEAD_DIM), lambda i, j: (0, 0)),
        ],
        out_specs=pl.BlockSpec((tm, tn), lambda i, j: (i, j)),
        out_shape=jax.ShapeDtypeStruct((t, n), BF16),
        scratch_shapes=[pltpu.VMEM((tm, d), BF16)],
        compiler_params=pltpu.CompilerParams(
            dimension_semantics=("parallel", "arbitrary"),
            vmem_limit_bytes=_vmem_limit(blocks, _nbytes((tm, d), BF16))),
        name="sb_in_proj",
    )(x, nw.reshape(1, d), w, qn_w.reshape(1, SB_HEAD_DIM), kn_w.reshape(1, SB_HEAD_DIM))


def _out_ple_kernel(a_ref, wo_ref, h_ref, nw_ref, gw_ref, p_ref, pw_ref, o_ref, h1_ref, xn_ref, *,
                    n_tiles):
    j = pl.program_id(1)
    tn = o_ref.shape[1]

    strips = _row_strips(o_ref.shape[0])

    @pl.when(j < n_tiles)
    def _():
        cols = pl.ds(pl.multiple_of(j * tn, tn), tn)
        for rows in strips:
            h1_ref[rows, cols] = h_ref[rows, :] + jnp.dot(a_ref[rows, :], wo_ref[...],
                                                          preferred_element_type=F32)

    @pl.when(j == n_tiles)
    def _():
        xn_ref[...] = _rms_normalize(h1_ref[...], nw_ref[...], NORM_EPS).astype(xn_ref.dtype)

    @pl.when(j >= n_tiles)
    def _():
        cols = pl.ds(pl.multiple_of((j - n_tiles) * tn, tn), tn)
        for rows in strips:
            gate = _sigmoid(jnp.dot(xn_ref[rows, :], gw_ref[...], preferred_element_type=F32))
            emb = jnp.dot(p_ref[rows, :].astype(BF16), pw_ref[...], preferred_element_type=F32)
            o_ref[rows, :] = h1_ref[rows, cols] + emb * gate


def _out_ple(a, wo, mixer_layer, h, nw, gw, p_all, pw, layer, tm_pref=1024, tn_pref=512):
    t, k = a.shape
    d = h.shape[1]
    pd = p_all.shape[1]
    tm, tn = _pick_tile(t, tm_pref), _pick_tile(d, tn_pref)
    nt = d // tn
    p_row0 = layer * (t // tm)
    first = lambda j: jnp.minimum(j, nt - 1)
    second = lambda j: jnp.maximum(j - nt, 0)
    blocks = [_nbytes((tm, k), BF16), _nbytes((k, tn), BF16), _nbytes((tm, tn), F32),
              _nbytes((d, tn), BF16), _nbytes((tm, pd), F32), _nbytes((pd, tn), BF16),
              _nbytes((tm, tn), F32)]
    scratch = _nbytes((tm, d), F32) + _nbytes((tm, d), BF16)
    return pl.pallas_call(
        functools.partial(_out_ple_kernel, n_tiles=nt),
        grid=(t // tm, 2 * nt),
        in_specs=[
            pl.BlockSpec((tm, k), lambda i, j: (i, 0)),
            pl.BlockSpec((None, k, tn), lambda i, j: (mixer_layer, 0, first(j))),
            pl.BlockSpec((tm, tn), lambda i, j: (i, first(j))),
            pl.BlockSpec((1, d), lambda i, j: (0, 0)),
            pl.BlockSpec((None, d, tn), lambda i, j: (layer, 0, second(j))),
            pl.BlockSpec((tm, pd), lambda i, j: (p_row0 + i, 0)),
            pl.BlockSpec((None, pd, tn), lambda i, j: (layer, 0, second(j))),
        ],
        out_specs=pl.BlockSpec((tm, tn), lambda i, j: (i, second(j))),
        out_shape=jax.ShapeDtypeStruct((t, d), F32),
        scratch_shapes=[pltpu.VMEM((tm, d), F32), pltpu.VMEM((tm, d), BF16)],
        compiler_params=pltpu.CompilerParams(
            dimension_semantics=("parallel", "arbitrary"),
            vmem_limit_bytes=_vmem_limit(blocks, scratch)),
        name="out_ple",
    )(a, wo, h, nw.reshape(1, d), gw, p_all, pw)


def _sb_attn_kernel(q_ref, k_ref, v_ref, g_ref, o_ref, *, blk, heads_per_step, strips):
    i = pl.program_id(2)
    hd = SB_HEAD_DIM
    row = lax.broadcasted_iota(jnp.int32, (blk, blk), 0)
    col = lax.broadcasted_iota(jnp.int32, (blk, blk), 1)
    strict = col < row
    later = (row > col).astype(BF16)

    rows = blk // strips
    units = [(slice(hh * hd, (hh + 1) * hd), slice(st * rows, (st + 1) * rows))
             for hh in range(heads_per_step) for st in range(strips)]

    def block(j, state, diagonal):
        start = pl.multiple_of(j * blk, blk)
        zs = [lax.dot_general(q_ref[rs, hs], k_ref[pl.ds(start, blk), hs], (((1,), (1,)), ((), ())),
                              preferred_element_type=F32) for hs, rs in units]
        logits, carries = [], []
        for u, z in enumerate(zs):
            neg_abs = lax.bitcast_convert_type(
                lax.bitcast_convert_type(z, jnp.uint32) | jnp.uint32(SIGN_BIT), F32)
            log_beta = jnp.minimum(z, 0.0) - jnp.log(1.0 + jnp.exp2(neg_abs)) * LOG2E
            l1m = log_beta - z
            if diagonal:
                l1m = jnp.where(strict[units[u][1], :], l1m, 0.0)
            rest = jnp.dot(l1m.astype(BF16), later, preferred_element_type=F32)
            logits.append(log_beta + rest + state[u][0])
            carries.append(state[u][0] + jnp.sum(l1m, axis=-1, keepdims=True))
        out = []
        for u, logit in enumerate(logits):
            att = jnp.exp2(logit)
            if diagonal:
                att = jnp.where(strict[units[u][1], :], att, 0.0)
            acc = state[u][1] + jnp.dot(att.astype(BF16), v_ref[pl.ds(start, blk), units[u][0]],
                                        preferred_element_type=F32)
            out.append((carries[u], acc))
        return tuple(out)

    init = tuple((jnp.zeros((rows, 1), F32), jnp.zeros((rows, hd), F32)) for _ in units)
    state = block(i, init, True)
    state = lax.fori_loop(0, i, lambda t, s: block(i - 1 - t, s, False), state)
    for u, (hs, rs) in enumerate(units):
        o_ref[rs, hs] = (state[u][1] * g_ref[rs, hs].astype(F32)).astype(o_ref.dtype)


def _sb_attention(qkvg, batch, seq, blk_pref=256, heads_per_step=8, strips=1):
    t, n = qkvg.shape
    heads = n // (4 * SB_HEAD_DIM)
    blk = _pick_tile(seq, blk_pref)
    nq = seq // blk
    assert heads % heads_per_step == 0
    hsteps = heads // heads_per_step
    w = heads_per_step * SB_HEAD_DIM
    blocks = [_nbytes((blk, w), BF16), 2 * _nbytes((seq, w), BF16), 2 * _nbytes((blk, w), BF16)]
    return pl.pallas_call(
        functools.partial(_sb_attn_kernel, blk=blk, heads_per_step=heads_per_step, strips=strips),
        grid=(batch, hsteps, nq),
        in_specs=[
            pl.BlockSpec((blk, w), lambda b, h, i: (b * nq + i, h)),
            pl.BlockSpec((seq, w), lambda b, h, i: (b, hsteps + h)),
            pl.BlockSpec((seq, w), lambda b, h, i: (b, 2 * hsteps + h)),
            pl.BlockSpec((blk, w), lambda b, h, i: (b * nq + i, 3 * hsteps + h)),
        ],
        out_specs=pl.BlockSpec((blk, w), lambda b, h, i: (b * nq + i, h)),
        out_shape=jax.ShapeDtypeStruct((t, heads * SB_HEAD_DIM), BF16),
        compiler_params=pltpu.CompilerParams(
            dimension_semantics=("parallel", "parallel", "arbitrary"),
            vmem_limit_bytes=_vmem_limit(blocks)),
        name="sb_attention",
    )(qkvg, qkvg, qkvg, qkvg)


def _split3(v):
    hi = v.astype(BF16)
    r = v - hi.astype(F32)
    mid = r.astype(BF16)
    lo = (r - mid.astype(F32)).astype(BF16)
    return hi, mid, lo


def _ssd_kernel(z_ref, x_ref, b_ref, c_ref, dt_ref, spread_ref, convw_ref, convb_ref, dtb_ref,
                alog_ref, dexp_ref, gnw_ref, o_ref,
                cbuf, state_ref, acum_sp_ref, ea_sp_ref, rowb_ref, wt_ref, *, d_inner):
    chunk = SSD_CHUNK
    heads_per_group = d_inner // SSD_N_GROUPS // SSD_HEAD_DIM
    gw = d_inner // SSD_N_GROUPS
    nbc = SSD_N_GROUPS * SSD_D_STATE
    first = pl.program_id(1) == 0

    @pl.when(first)
    def _():
        cbuf[0:SUBLANES, :] = jnp.zeros((SUBLANES, cbuf.shape[1]), F32)
        state_ref[...] = jnp.zeros(state_ref.shape, F32)

    cbuf[SUBLANES:SUBLANES + chunk, 0:d_inner] = x_ref[...].astype(F32)
    cbuf[SUBLANES:SUBLANES + chunk, d_inner:d_inner + nbc] = b_ref[...].astype(F32)
    cbuf[SUBLANES:SUBLANES + chunk, d_inner + nbc:d_inner + 2 * nbc] = c_ref[...].astype(F32)

    dt = _softplus(dt_ref[...] + dtb_ref[...])
    adt = dt * (-LOG2E * jnp.exp(alog_ref[...]))
    li = lax.broadcasted_iota(jnp.int32, (chunk, chunk), 0)
    si = lax.broadcasted_iota(jnp.int32, (chunk, chunk), 1)
    causal = si <= li
    tri = causal.astype(BF16)
    acum = sum(jnp.dot(tri, piece, preferred_element_type=F32) for piece in _split3(adt))
    acum_t = acum.T
    dt_t = dt.T
    rowb_ref[...] = acum_t - jnp.log(dt_t) * LOG2E
    wt_ref[...] = dt_t * jnp.exp2(acum_t[:, chunk - 1:chunk] - acum_t)
    acum_sp_ref[...] = sum(jnp.dot(piece, spread_ref[...], preferred_element_type=F32)
                           for piece in _split3(acum))
    ea_sp_ref[...] = sum(jnp.dot(piece, spread_ref[...], preferred_element_type=F32)
                         for piece in _split3(jnp.exp2(acum)))

    lane_lo = lax.broadcasted_iota(jnp.int32, (chunk, LANES), 1) < SSD_HEAD_DIM

    def conv_silu(col, width):
        cols = pl.ds(col, width)
        w = convw_ref[:, cols]
        acc = convb_ref[:, cols]
        for tap in range(SSD_D_CONV):
            shift = SSD_D_CONV - 1 - tap
            acc = acc + w[tap:tap + 1, :] * cbuf[pl.ds(SUBLANES - shift, chunk), cols]
        return acc * _sigmoid(acc)

    def group_body(g, _):
        xcol = pl.multiple_of(g * gw, gw)
        ncol = pl.multiple_of(g * SSD_D_STATE, SSD_D_STATE)
        xs = conv_silu(xcol, gw)
        bg = conv_silu(d_inner + ncol, SSD_D_STATE)
        cg = conv_silu(d_inner + nbc + ncol, SSD_D_STATE)
        scores = lax.dot_general(cg.astype(BF16), bg.astype(BF16), (((1,), (1,)), ((), ())),
                                 preferred_element_type=F32)
        bg_t = bg.T
        hrow = pl.multiple_of(g * SUBLANES, SUBLANES)
        acum_g = acum_sp_ref[:, pl.ds(ncol, LANES)]
        ea_g = ea_sp_ref[:, pl.ds(ncol, LANES)]
        rowb_g = rowb_ref[pl.ds(hrow, heads_per_group), :]
        wt_g = wt_ref[pl.ds(hrow, heads_per_group), :]
        st = state_ref[g]
        y_off = jnp.dot(cg.astype(BF16), st.astype(BF16), preferred_element_type=F32)

        pieces = []
        for pr in range(heads_per_group // 2):
            lhs_diag, lhs_state, ea_cols = [], [], []
            for r in (2 * pr, 2 * pr + 1):
                colb = jnp.broadcast_to(acum_g[:, r:r + 1], (chunk, chunk))
                decay_dt = jnp.where(causal, jnp.exp2(colb - rowb_g[r:r + 1, :]), 0.0)
                lhs_diag.append((decay_dt * scores).astype(BF16))
                lhs_state.append((bg_t * wt_g[r:r + 1, :]).astype(BF16))
                ea_cols.append(jnp.broadcast_to(ea_g[:, r:r + 1], (chunk, LANES)))
            psl = slice(pr * LANES, (pr + 1) * LANES)
            xp = xs[:, psl]
            x_bd = jnp.concatenate([jnp.where(lane_lo, xp, 0.0).astype(BF16),
                                    jnp.where(lane_lo, 0.0, xp).astype(BF16)], axis=0)
            ea_pair = jnp.where(lane_lo, ea_cols[0], ea_cols[1])
            y_pair = (jnp.dot(jnp.concatenate(lhs_diag, axis=1), x_bd, preferred_element_type=F32)
                      + ea_pair * y_off[:, psl])
            contrib = jnp.dot(jnp.concatenate(lhs_state, axis=1), x_bd, preferred_element_type=F32)
            state_ref[g, :, psl] = st[:, psl] * ea_pair[chunk - 1:chunk, :] + contrib
            pieces.append(y_pair)

        xsl = pl.ds(xcol, gw)
        y = jnp.concatenate(pieces, axis=1) + dexp_ref[:, xsl] * xs
        zt = z_ref[:, xsl].astype(F32)
        y = y * (zt * _sigmoid(zt))
        o_ref[:, xsl] = _rms_normalize(y, gnw_ref[:, xsl], GATED_NORM_EPS).astype(o_ref.dtype)
        return 0

    lax.fori_loop(0, SSD_N_GROUPS, group_body, 0)

    cbuf[0:SUBLANES, :] = cbuf[chunk:chunk + SUBLANES, :]


def _ssd_mixer(proj, dt_raw, conv_w, conv_b, dt_bias_pad, a_log_pad, d_exp, gnorm_w, batch, seq,
               d_inner):
    t = proj.shape[0]
    chunk = SSD_CHUNK
    nc = seq // chunk
    nbc = SSD_N_GROUPS * SSD_D_STATE
    conv_dim = d_inner + 2 * nbc
    gw = d_inner // SSD_N_GROUPS
    heads_per_group = gw // SSD_HEAD_DIM
    assert heads_per_group == SUBLANES and d_inner % nbc == 0
    hh = jnp.arange(LANES)[:, None]
    cc = jnp.arange(SSD_N_GROUPS * LANES)[None, :]
    spread = ((cc % LANES < heads_per_group)
              & (hh == (cc // LANES) * heads_per_group + cc % LANES)).astype(BF16)
    row = lambda b, c: b * nc + c
    blocks = [2 * _nbytes((chunk, d_inner), proj.dtype), 2 * _nbytes((chunk, nbc), proj.dtype),
              _nbytes((chunk, LANES), F32), _nbytes(spread.shape, BF16),
              _nbytes((8, conv_dim), F32) * 2, _nbytes((chunk, d_inner), BF16)]
    scratch = (_nbytes((chunk + SUBLANES, conv_dim), F32) + _nbytes((SSD_N_GROUPS, SSD_D_STATE, gw), F32)
               + 2 * _nbytes((chunk, SSD_N_GROUPS * LANES), F32) + 2 * _nbytes((LANES, chunk), F32))
    full = lambda shape: pl.BlockSpec(shape, lambda b, c: (0, 0))
    return pl.pallas_call(
        functools.partial(_ssd_kernel, d_inner=d_inner),
        grid=(batch, nc),
        in_specs=[
            pl.BlockSpec((chunk, d_inner), lambda b, c: (row(b, c), 0)),
            pl.BlockSpec((chunk, d_inner), lambda b, c: (row(b, c), 1)),
            pl.BlockSpec((chunk, nbc), lambda b, c: (row(b, c), 2 * d_inner // nbc)),
            pl.BlockSpec((chunk, nbc), lambda b, c: (row(b, c), 2 * d_inner // nbc + 1)),
            pl.BlockSpec((chunk, LANES), lambda b, c: (row(b, c), 0)),
            full(spread.shape),
            full((SSD_D_CONV, conv_dim)),
            full((1, conv_dim)),
            full((1, LANES)),
            full((1, LANES)),
            full((1, d_inner)),
            full((1, d_inner)),
        ],
        out_specs=pl.BlockSpec((chunk, d_inner), lambda b, c: (row(b, c), 0)),
        out_shape=jax.ShapeDtypeStruct((t, d_inner), BF16),
        scratch_shapes=[
            pltpu.VMEM((chunk + SUBLANES, conv_dim), F32),
            pltpu.VMEM((SSD_N_GROUPS, SSD_D_STATE, gw), F32),
            pltpu.VMEM((chunk, SSD_N_GROUPS * LANES), F32),
            pltpu.VMEM((chunk, SSD_N_GROUPS * LANES), F32),
            pltpu.VMEM((LANES, chunk), F32),
            pltpu.VMEM((LANES, chunk), F32),
        ],
        compiler_params=pltpu.CompilerParams(
            dimension_semantics=("parallel", "arbitrary"),
            vmem_limit_bytes=_vmem_limit(blocks, scratch)),
        name="ssd_mixer",
    )(proj, proj, proj, proj, dt_raw, spread, conv_w, conv_b.reshape(1, conv_dim),
      dt_bias_pad, a_log_pad, d_exp, gnorm_w.reshape(1, d_inner))


def _pad_lanes(v):
    return jnp.pad(v.astype(F32), (0, LANES - v.shape[0])).reshape(1, LANES)


def _ssd_mix(h, nw, in_w, dt_w, layer, conv_w, conv_b, dt_bias, a_log, d_skip, gnorm_w, d_inner,
             batch, seq):
    main = 2 * d_inner + 2 * SSD_N_GROUPS * SSD_D_STATE
    proj = _norm_matmul(h, nw, in_w, layer, main, BF16)
    dt_raw = _norm_matmul(h, nw, dt_w, layer, LANES, F32)
    d_exp = jnp.repeat(d_skip.astype(F32), SSD_HEAD_DIM).reshape(1, d_inner)
    return _ssd_mixer(proj, dt_raw, conv_w, conv_b, _pad_lanes(dt_bias), _pad_lanes(a_log), d_exp,
                      gnorm_w, batch, seq, d_inner)


def kernel(x, p, norm_w, ssd_in_w, ssd_conv_w, ssd_conv_b, ssd_dt_bias, ssd_a_log, ssd_d, ssd_gnorm_w, ssd_out_w, sb_in_w, sb_qn_w, sb_kn_w, sb_out_w, ple_norm_w, ple_gate_w, ple_proj_w):
    batch, seq, d_model = x.shape
    depth = p.shape[0]
    d_inner = ssd_out_w.shape[1]
    main = 2 * d_inner + 2 * SSD_N_GROUPS * SSD_D_STATE
    n_ssd_heads = ssd_in_w.shape[2] - main
    ssd_in_b = ssd_in_w.astype(BF16)
    ssd_dt_b = jnp.pad(ssd_in_w[:, :, main:], ((0, 0), (0, 0), (0, LANES - n_ssd_heads))).astype(BF16)
    ssd_out_b = ssd_out_w.astype(BF16)
    sb_in_b = sb_in_w.astype(BF16)
    sb_out_b = sb_out_w.astype(BF16)
    gate_b = ple_gate_w.astype(BF16)
    proj_b = ple_proj_w.astype(BF16)
    p_all = p.reshape(depth * batch * seq, p.shape[-1])
    h = x.reshape(batch * seq, d_model)
    for i in range(depth):
        j = i // 2
        if i % 2 == 0:
            a = _ssd_mix(h, norm_w[i], ssd_in_b, ssd_dt_b, j, ssd_conv_w[j], ssd_conv_b[j],
                         ssd_dt_bias[j], ssd_a_log[j], ssd_d[j], ssd_gnorm_w[j], d_inner, batch, seq)
            wo = ssd_out_b
        else:
            qkvg = _sb_in_proj(h, norm_w[i], sb_in_b, j, sb_qn_w[j], sb_kn_w[j])
            a = _sb_attention(qkvg, batch, seq)
            wo = sb_out_b
        h = _out_ple(a, wo, j, h, ple_norm_w[i], gate_b, p_all, proj_b, i)
    return h.reshape(batch, seq, d_model)
```

```python
import functools
import math

import jax
import jax.numpy as jnp
from jax import lax
from jax.experimental import pallas as pl
from jax.experimental.pallas import tpu as pltpu

F32 = jnp.float32
BF16 = jnp.bfloat16

NORM_EPS = 1e-6
GATED_NORM_EPS = 1e-5
LOG2E = 1.4426950408889634
SIGN_BIT = 0x80000000

SSD_HEAD_DIM = 64
SSD_N_GROUPS = 8
SSD_D_STATE = 128
SSD_D_CONV = 4
SSD_CHUNK = 128
SB_HEAD_DIM = 128

LANES = 128
SUBLANES = 8
BF16_SUBLANES = 16
V7X_VMEM_BYTES = 64 * 1024 * 1024
COMPILER_SCRATCH_BYTES = 12 * 1024 * 1024


def _vmem_limit(block_bytes, scratch_bytes=0):
    need = 2 * sum(block_bytes) + scratch_bytes + COMPILER_SCRATCH_BYTES
    return int(min(need, V7X_VMEM_BYTES - 8 * 1024 * 1024))


def _nbytes(shape, dtype):
    return math.prod(shape) * jnp.dtype(dtype).itemsize


def _pick_tile(n, pref):
    t = min(n, pref)
    assert n % t == 0, (n, t)
    return t


def _sigmoid(x):
    return 1.0 / (1.0 + jnp.exp(-x))


def _softplus(x):
    return jnp.maximum(x, 0.0) + jnp.log(1.0 + jnp.exp(-jnp.abs(x)))


def _rms_normalize(x, w, eps):
    ms = jnp.mean(x * x, axis=-1, keepdims=True)
    return x * lax.rsqrt(ms + eps) * w


def _row_block_ahead(i, done_with_block, n_blocks):
    return jnp.where(done_with_block, jnp.minimum(i + 1, n_blocks - 1), i)


STRIP_ROWS = 256


def _row_strips(rows):
    step = min(rows, STRIP_ROWS)
    return [slice(r, r + step) for r in range(0, rows, step)]


def _norm_matmul_kernel(x_ref, nw_ref, w_ref, o_ref, xn_ref):
    @pl.when(pl.program_id(1) == 0)
    def _():
        xn_ref[...] = _rms_normalize(x_ref[...], nw_ref[...], NORM_EPS).astype(xn_ref.dtype)

    o_ref[...] = jnp.dot(xn_ref[...], w_ref[...], preferred_element_type=F32).astype(o_ref.dtype)


def _norm_matmul(x, nw, w, layer, n, out_dtype, tm_pref=1024, tn_pref=1024):
    t, d = x.shape
    tm, tn = _pick_tile(t, tm_pref), _pick_tile(n, tn_pref)
    blocks = [_nbytes((tm, d), F32), _nbytes((d, tn), BF16), _nbytes((tm, tn), out_dtype)]
    return pl.pallas_call(
        _norm_matmul_kernel,
        grid=(t // tm, n // tn),
        in_specs=[
            pl.BlockSpec((tm, d), lambda i, j: (_row_block_ahead(i, j >= 1, t // tm), 0)),
            pl.BlockSpec((1, d), lambda i, j: (0, 0)),
            pl.BlockSpec((None, d, tn), lambda i, j: (layer, 0, j)),
        ],
        out_specs=pl.BlockSpec((tm, tn), lambda i, j: (i, j)),
        out_shape=jax.ShapeDtypeStruct((t, n), out_dtype),
        scratch_shapes=[pltpu.VMEM((tm, d), BF16)],
        compiler_params=pltpu.CompilerParams(
            dimension_semantics=("parallel", "arbitrary"),
            vmem_limit_bytes=_vmem_limit(blocks, _nbytes((tm, d), BF16))),
        name="norm_matmul",
    )(x, nw.reshape(1, d), w)


def _sb_in_proj_kernel(x_ref, nw_ref, w_ref, qn_ref, kn_ref, o_ref, xn_ref, *, tiles_per_section):
    j = pl.program_id(1)

    @pl.when(j == 0)
    def _():
        xn_ref[...] = _rms_normalize(x_ref[...], nw_ref[...], NORM_EPS).astype(xn_ref.dtype)

    section = j // tiles_per_section
    heads_per_tile = o_ref.shape[1] // SB_HEAD_DIM

    def project(epilogue):
        for rows in _row_strips(o_ref.shape[0]):
            acc = jnp.dot(xn_ref[rows, :], w_ref[...], preferred_element_type=F32)
            o_ref[rows, :] = epilogue(acc).astype(o_ref.dtype)

    def head_norm(hw_ref, scale):
        def epilogue(acc):
            heads = [_rms_normalize(acc[:, hh * SB_HEAD_DIM:(hh + 1) * SB_HEAD_DIM], hw_ref[...],
                                    NORM_EPS) * scale for hh in range(heads_per_tile)]
            return jnp.concatenate(heads, axis=1)
        return epilogue

    @pl.when(section == 0)
    def _():
        project(head_norm(qn_ref, LOG2E / math.sqrt(SB_HEAD_DIM)))

    @pl.when(section == 1)
    def _():
        project(head_norm(kn_ref, 1.0))

    @pl.when(section == 2)
    def _():
        project(lambda acc: acc)

    @pl.when(section == 3)
    def _():
        project(lambda acc: acc * _sigmoid(acc))


def _sb_in_proj(x, nw, w, layer, qn_w, kn_w, tm_pref=1024, tn_pref=1024):
    t, d = x.shape
    n = w.shape[2]
    width = n // 4
    tm, tn = _pick_tile(t, tm_pref), _pick_tile(width, tn_pref)
    blocks = [_nbytes((tm, d), F32), _nbytes((d, tn), BF16), _nbytes((tm, tn), BF16)]
    return pl.pallas_call(
        functools.partial(_sb_in_proj_kernel, tiles_per_section=width // tn),
        grid=(t // tm, n // tn),
        in_specs=[
            pl.BlockSpec((tm, d), lambda i, j: (_row_block_ahead(i, j >= 1, t // tm), 0)),
            pl.BlockSpec((1, d), lambda i, j: (0, 0)),
            pl.BlockSpec((None, d, tn), lambda i, j: (layer, 0, j)),
            pl.BlockSpec((1, SB_HEAD_DIM), lambda i, j: (0, 0)),
            pl.BlockSpec((1, SB_HEAD_DIM), lambda i, j: (0, 0)),
        ],
        out_specs=pl.BlockSpec((tm, tn), lambda i, j: (i, j)),
        out_shape=jax.ShapeDtypeStruct((t, n), BF16),
        scratch_shapes=[pltpu.VMEM((tm, d), BF16)],
        compiler_params=pltpu.CompilerParams(
            dimension_semantics=("parallel", "arbitrary"),
            vmem_limit_bytes=_vmem_limit(blocks, _nbytes((tm, d), BF16))),
        name="sb_in_proj",
    )(x, nw.reshape(1, d), w, qn_w.reshape(1, SB_HEAD_DIM), kn_w.reshape(1, SB_HEAD_DIM))


def _out_ple_kernel(a_ref, wo_ref, h_ref, nw_ref, gw_ref, p_ref, pw_ref, o_ref, h1_ref, xn_ref, *,
                    n_tiles):
    j = pl.program_id(1)
    tn = o_ref.shape[1]

    strips = _row_strips(o_ref.shape[0])

    @pl.when(j < n_tiles)
    def _():
        cols = pl.ds(pl.multiple_of(j * tn, tn), tn)
        for rows in strips:
            h1_ref[rows, cols] = h_ref[rows, :] + jnp.dot(a_ref[rows, :], wo_ref[...],
                                                          preferred_element_type=F32)

    @pl.when(j == n_tiles)
    def _():
        xn_ref[...] = _rms_normalize(h1_ref[...], nw_ref[...], NORM_EPS).astype(xn_ref.dtype)

    @pl.when(j >= n_tiles)
    def _():
        cols = pl.ds(pl.multiple_of((j - n_tiles) * tn, tn), tn)
        for rows in strips:
            gate = _sigmoid(jnp.dot(xn_ref[rows, :], gw_ref[...], preferred_element_type=F32))
            emb = jnp.dot(p_ref[rows, :].astype(BF16), pw_ref[...], preferred_element_type=F32)
            o_ref[rows, :] = h1_ref[rows, cols] + emb * gate


def _out_ple(a, wo, mixer_layer, h, nw, gw, p_all, pw, layer, tm_pref=1024, tn_pref=512):
    t, k = a.shape
    d = h.shape[1]
    pd = p_all.shape[1]
    tm, tn = _pick_tile(t, tm_pref), _pick_tile(d, tn_pref)
    nt = d // tn
    p_row0 = layer * (t // tm)
    first = lambda j: jnp.minimum(j, nt - 1)
    second = lambda j: jnp.maximum(j - nt, 0)
    blocks = [_nbytes((tm, k), BF16), _nbytes((k, tn), BF16), _nbytes((tm, tn), F32),
              _nbytes((d, tn), BF16), _nbytes((tm, pd), F32), _nbytes((pd, tn), BF16),
              _nbytes((tm, tn), F32)]
    scratch = _nbytes((tm, d), F32) + _nbytes((tm, d), BF16)
    return pl.pallas_call(
        functools.partial(_out_ple_kernel, n_tiles=nt),
        grid=(t // tm, 2 * nt),
        in_specs=[
            pl.BlockSpec((tm, k), lambda i, j: (_row_block_ahead(i, j >= nt, t // tm), 0)),
            pl.BlockSpec((None, k, tn), lambda i, j: (mixer_layer, 0, first(j))),
            pl.BlockSpec((tm, tn), lambda i, j: (i, first(j))),
            pl.BlockSpec((1, d), lambda i, j: (0, 0)),
            pl.BlockSpec((None, d, tn), lambda i, j: (layer, 0, second(j))),
            pl.BlockSpec((tm, pd), lambda i, j: (p_row0 + i, 0)),
            pl.BlockSpec((None, pd, tn), lambda i, j: (layer, 0, second(j))),
        ],
        out_specs=pl.BlockSpec((tm, tn), lambda i, j: (i, second(j))),
        out_shape=jax.ShapeDtypeStruct((t, d), F32),
        scratch_shapes=[pltpu.VMEM((tm, d), F32), pltpu.VMEM((tm, d), BF16)],
        compiler_params=pltpu.CompilerParams(
            dimension_semantics=("parallel", "arbitrary"),
            vmem_limit_bytes=_vmem_limit(blocks, scratch)),
        name="out_ple",
    )(a, wo, h, nw.reshape(1, d), gw, p_all, pw)


def _sb_attn_kernel(q_ref, k_ref, v_ref, g_ref, o_ref, *, blk, heads_per_step, strips):
    i = pl.program_id(2)
    hd = SB_HEAD_DIM
    row = lax.broadcasted_iota(jnp.int32, (blk, blk), 0)
    col = lax.broadcasted_iota(jnp.int32, (blk, blk), 1)
    strict = col < row
    later = (row > col).astype(BF16)

    rows = blk // strips
    units = [(slice(hh * hd, (hh + 1) * hd), slice(st * rows, (st + 1) * rows))
             for hh in range(heads_per_step) for st in range(strips)]

    def block(j, state, diagonal):
        start = pl.multiple_of(j * blk, blk)
        zs = [lax.dot_general(q_ref[rs, hs], k_ref[pl.ds(start, blk), hs], (((1,), (1,)), ((), ())),
                              preferred_element_type=F32) for hs, rs in units]
        logits, carries = [], []
        for u, z in enumerate(zs):
            neg_abs = lax.bitcast_convert_type(
                lax.bitcast_convert_type(z, jnp.uint32) | jnp.uint32(SIGN_BIT), F32)
            log_beta = jnp.minimum(z, 0.0) - jnp.log(1.0 + jnp.exp2(neg_abs)) * LOG2E
            l1m = log_beta - z
            if diagonal:
                l1m = jnp.where(strict[units[u][1], :], l1m, 0.0)
            rest = jnp.dot(l1m.astype(BF16), later, preferred_element_type=F32)
            logits.append(log_beta + rest + state[u][0])
            carries.append(state[u][0] + jnp.sum(l1m, axis=-1, keepdims=True))
        out = []
        for u, logit in enumerate(logits):
            att = jnp.exp2(logit)
            if diagonal:
                att = jnp.where(strict[units[u][1], :], att, 0.0)
            acc = state[u][1] + jnp.dot(att.astype(BF16), v_ref[pl.ds(start, blk), units[u][0]],
                                        preferred_element_type=F32)
            out.append((carries[u], acc))
        return tuple(out)

    init = tuple((jnp.zeros((rows, 1), F32), jnp.zeros((rows, hd), F32)) for _ in units)
    state = block(i, init, True)
    state = lax.fori_loop(0, i, lambda t, s: block(i - 1 - t, s, False), state)
    for u, (hs, rs) in enumerate(units):
        o_ref[rs, hs] = (state[u][1] * g_ref[rs, hs].astype(F32)).astype(o_ref.dtype)


def _sb_attention(qkvg, batch, seq, blk_pref=256, heads_per_step=8, strips=1):
    t, n = qkvg.shape
    heads = n // (4 * SB_HEAD_DIM)
    blk = _pick_tile(seq, blk_pref)
    nq = seq // blk
    assert heads % heads_per_step == 0
    hsteps = heads // heads_per_step
    w = heads_per_step * SB_HEAD_DIM
    blocks = [_nbytes((blk, w), BF16), 2 * _nbytes((seq, w), BF16), 2 * _nbytes((blk, w), BF16)]
    return pl.pallas_call(
        functools.partial(_sb_attn_kernel, blk=blk, heads_per_step=heads_per_step, strips=strips),
        grid=(batch, hsteps, nq),
        in_specs=[
            pl.BlockSpec((blk, w), lambda b, h, i: (b * nq + i, h)),
            pl.BlockSpec((seq, w), lambda b, h, i: (b, hsteps + h)),
            pl.BlockSpec((seq, w), lambda b, h, i: (b, 2 * hsteps + h)),
            pl.BlockSpec((blk, w), lambda b, h, i: (b * nq + i, 3 * hsteps + h)),
        ],
        out_specs=pl.BlockSpec((blk, w), lambda b, h, i: (b * nq + i, h)),
        out_shape=jax.ShapeDtypeStruct((t, heads * SB_HEAD_DIM), BF16),
        compiler_params=pltpu.CompilerParams(
            dimension_semantics=("parallel", "parallel", "arbitrary"),
            vmem_limit_bytes=_vmem_limit(blocks)),
        name="sb_attention",
    )(qkvg, qkvg, qkvg, qkvg)


def _split3(v):
    hi = v.astype(BF16)
    r = v - hi.astype(F32)
    mid = r.astype(BF16)
    lo = (r - mid.astype(F32)).astype(BF16)
    return hi, mid, lo


def _ssd_kernel(z_ref, x_ref, b_ref, c_ref, dt_ref, spread_ref, shift_ref, convw_ref, convb_ref,
                dtb_ref, alog_ref, dexp_ref, gnw_ref, o_ref,
                tail_ref, state_ref, acum_sp_ref, ea_sp_ref, rowb_ref, wt_ref, *, d_inner):
    chunk = SSD_CHUNK
    heads_per_group = d_inner // SSD_N_GROUPS // SSD_HEAD_DIM
    gw = d_inner // SSD_N_GROUPS
    nbc = SSD_N_GROUPS * SSD_D_STATE
    first = pl.program_id(1) == 0

    @pl.when(first)
    def _():
        tail_ref[...] = jnp.zeros(tail_ref.shape, tail_ref.dtype)
        state_ref[...] = jnp.zeros(state_ref.shape, F32)

    dt = _softplus(dt_ref[...] + dtb_ref[...])
    adt = dt * (-LOG2E * jnp.exp(alog_ref[...]))
    li = lax.broadcasted_iota(jnp.int32, (chunk, chunk), 0)
    si = lax.broadcasted_iota(jnp.int32, (chunk, chunk), 1)
    causal = si <= li
    tri = causal.astype(BF16)
    acum = sum(jnp.dot(tri, piece, preferred_element_type=F32) for piece in _split3(adt))
    acum_t = acum.T
    dt_t = dt.T
    rowb_ref[...] = acum_t - jnp.log(dt_t) * LOG2E
    wt_ref[...] = dt_t * jnp.exp2(acum_t[:, chunk - 1:chunk] - acum_t)
    acum_sp_ref[...] = sum(jnp.dot(piece, spread_ref[...], preferred_element_type=F32)
                           for piece in _split3(acum))
    ea_sp_ref[...] = sum(jnp.dot(piece, spread_ref[...], preferred_element_type=F32)
                         for piece in _split3(jnp.exp2(acum)))

    lane_lo = lax.broadcasted_iota(jnp.int32, (chunk, LANES), 1) < SSD_HEAD_DIM

    def conv_silu(g):
        xcol = g * gw
        ncol = g * SSD_D_STATE
        srcs = ((x_ref, xcol, xcol, gw), (b_ref, ncol, d_inner + ncol, SSD_D_STATE),
                (c_ref, ncol, d_inner + nbc + ncol, SSD_D_STATE))
        raw = jnp.concatenate(
            [jnp.concatenate([tail_ref[:, pl.ds(tcol, width)], ref[:, pl.ds(col, width)]], axis=0)
             for ref, col, tcol, width in srcs], axis=1)
        w = jnp.concatenate([convw_ref[:, pl.ds(tcol, width)] for _, _, tcol, width in srcs], axis=1)
        acc = jnp.concatenate([convb_ref[:, pl.ds(tcol, width)] for _, _, tcol, width in srcs], axis=1)
        shifted = jnp.dot(shift_ref[...], raw, preferred_element_type=F32)
        for tap in range(SSD_D_CONV):
            k = SSD_D_CONV - 1 - tap
            acc = acc + w[tap:tap + 1, :] * shifted[k * chunk:(k + 1) * chunk, :]
        return acc * _sigmoid(acc)

    def scan_group(g, act):
        xcol = g * gw
        ncol = g * SSD_D_STATE
        xs = act[:, 0:gw]
        bg = act[:, gw:gw + SSD_D_STATE]
        cg = act[:, gw + SSD_D_STATE:gw + 2 * SSD_D_STATE]
        scores = lax.dot_general(cg.astype(BF16), bg.astype(BF16), (((1,), (1,)), ((), ())),
                                 preferred_element_type=F32)
        bg_t = bg.T
        hrow = g * SUBLANES
        acum_g = acum_sp_ref[:, pl.ds(ncol, LANES)]
        ea_g = ea_sp_ref[:, pl.ds(ncol, LANES)]
        rowb_g = rowb_ref[pl.ds(hrow, heads_per_group), :]
        wt_g = wt_ref[pl.ds(hrow, heads_per_group), :]
        st = state_ref[g]
        y_off = jnp.dot(cg.astype(BF16), st.astype(BF16), preferred_element_type=F32)

        pieces = []
        for pr in range(heads_per_group // 2):
            lhs_diag, lhs_state, ea_cols = [], [], []
            for r in (2 * pr, 2 * pr + 1):
                colb = jnp.broadcast_to(acum_g[:, r:r + 1], (chunk, chunk))
                decay_dt = jnp.where(causal, jnp.exp2(colb - rowb_g[r:r + 1, :]), 0.0)
                lhs_diag.append((decay_dt * scores).astype(BF16))
                lhs_state.append((bg_t * wt_g[r:r + 1, :]).astype(BF16))
                ea_cols.append(jnp.broadcast_to(ea_g[:, r:r + 1], (chunk, LANES)))
            psl = slice(pr * LANES, (pr + 1) * LANES)
            xp = xs[:, psl]
            x_bd = jnp.concatenate([jnp.where(lane_lo, xp, 0.0).astype(BF16),
                                    jnp.where(lane_lo, 0.0, xp).astype(BF16)], axis=0)
            ea_pair = jnp.where(lane_lo, ea_cols[0], ea_cols[1])
            y_pair = (jnp.dot(jnp.concatenate(lhs_diag, axis=1), x_bd, preferred_element_type=F32)
                      + ea_pair * y_off[:, psl])
            contrib = jnp.dot(jnp.concatenate(lhs_state, axis=1), x_bd, preferred_element_type=F32)
            state_ref[g, :, psl] = st[:, psl] * ea_pair[chunk - 1:chunk, :] + contrib
            pieces.append(y_pair)

        xsl = pl.ds(xcol, gw)
        y = jnp.concatenate(pieces, axis=1) + dexp_ref[:, xsl] * xs
        zt = z_ref[:, xsl].astype(F32)
        y = y * (zt * _sigmoid(zt))
        o_ref[:, xsl] = _rms_normalize(y, gnw_ref[:, xsl], GATED_NORM_EPS).astype(o_ref.dtype)

    act = conv_silu(0)
    for g in range(SSD_N_GROUPS):
        act_next = conv_silu(g + 1) if g + 1 < SSD_N_GROUPS else None
        scan_group(g, act)
        act = act_next

    keep = slice(chunk - BF16_SUBLANES, chunk)
    tail_ref[keep, 0:d_inner] = x_ref[keep, :]
    tail_ref[keep, d_inner:d_inner + nbc] = b_ref[keep, :]
    tail_ref[keep, d_inner + nbc:d_inner + 2 * nbc] = c_ref[keep, :]


def _ssd_mixer(proj, dt_raw, conv_w, conv_b, dt_bias_pad, a_log_pad, d_exp, gnorm_w, batch, seq,
               d_inner):
    t = proj.shape[0]
    chunk = SSD_CHUNK
    nc = seq // chunk
    nbc = SSD_N_GROUPS * SSD_D_STATE
    conv_dim = d_inner + 2 * nbc
    gw = d_inner // SSD_N_GROUPS
    heads_per_group = gw // SSD_HEAD_DIM
    assert heads_per_group == SUBLANES and d_inner % nbc == 0
    assert proj.dtype == BF16
    hh = jnp.arange(LANES)[:, None]
    cc = jnp.arange(SSD_N_GROUPS * LANES)[None, :]
    spread = ((cc % LANES < heads_per_group)
              & (hh == (cc // LANES) * heads_per_group + cc % LANES)).astype(BF16)
    rr = jnp.arange(SSD_D_CONV * chunk)[:, None]
    shift = (jnp.arange(2 * chunk)[None, :] == chunk + rr % chunk - rr // chunk).astype(BF16)
    row = lambda b, c: b * nc + c
    blocks = [2 * _nbytes((chunk, d_inner), proj.dtype), 2 * _nbytes((chunk, nbc), proj.dtype),
              _nbytes((chunk, LANES), F32), _nbytes(spread.shape, BF16), _nbytes(shift.shape, BF16),
              _nbytes((8, conv_dim), F32) * 2, _nbytes((chunk, d_inner), BF16)]
    scratch = (_nbytes((chunk, conv_dim), BF16) + _nbytes((SSD_N_GROUPS, SSD_D_STATE, gw), F32)
               + 2 * _nbytes((chunk, SSD_N_GROUPS * LANES), F32) + 2 * _nbytes((LANES, chunk), F32))
    full = lambda shape: pl.BlockSpec(shape, lambda b, c: (0, 0))
    return pl.pallas_call(
        functools.partial(_ssd_kernel, d_inner=d_inner),
        grid=(batch, nc),
        in_specs=[
            pl.BlockSpec((chunk, d_inner), lambda b, c: (row(b, c), 0)),
            pl.BlockSpec((chunk, d_inner), lambda b, c: (row(b, c), 1)),
            pl.BlockSpec((chunk, nbc), lambda b, c: (row(b, c), 2 * d_inner // nbc)),
            pl.BlockSpec((chunk, nbc), lambda b, c: (row(b, c), 2 * d_inner // nbc + 1)),
            pl.BlockSpec((chunk, LANES), lambda b, c: (row(b, c), 0)),
            full(spread.shape),
            full(shift.shape),
            full((SSD_D_CONV, conv_dim)),
            full((1, conv_dim)),
            full((1, LANES)),
            full((1, LANES)),
            full((1, d_inner)),
            full((1, d_inner)),
        ],
        out_specs=pl.BlockSpec((chunk, d_inner), lambda b, c: (row(b, c), 0)),
        out_shape=jax.ShapeDtypeStruct((t, d_inner), BF16),
        scratch_shapes=[
            pltpu.VMEM((chunk, conv_dim), proj.dtype),
            pltpu.VMEM((SSD_N_GROUPS, SSD_D_STATE, gw), F32),
            pltpu.VMEM((chunk, SSD_N_GROUPS * LANES), F32),
            pltpu.VMEM((chunk, SSD_N_GROUPS * LANES), F32),
            pltpu.VMEM((LANES, chunk), F32),
            pltpu.VMEM((LANES, chunk), F32),
        ],
        compiler_params=pltpu.CompilerParams(
            dimension_semantics=("parallel", "arbitrary"),
            vmem_limit_bytes=_vmem_limit(blocks, scratch)),
        name="ssd_mixer",
    )(proj, proj, proj, proj, dt_raw, spread, shift, conv_w, conv_b.reshape(1, conv_dim),
      dt_bias_pad, a_log_pad, d_exp, gnorm_w.reshape(1, d_inner))


def _pad_lanes(v):
    return jnp.pad(v.astype(F32), (0, LANES - v.shape[0])).reshape(1, LANES)


def _ssd_mix(h, nw, in_w, dt_w, layer, conv_w, conv_b, dt_bias, a_log, d_skip, gnorm_w, d_inner,
             batch, seq):
    main = 2 * d_inner + 2 * SSD_N_GROUPS * SSD_D_STATE
    proj = _norm_matmul(h, nw, in_w, layer, main, BF16)
    dt_raw = _norm_matmul(h, nw, dt_w, layer, LANES, F32)
    d_exp = jnp.repeat(d_skip.astype(F32), SSD_HEAD_DIM).reshape(1, d_inner)
    return _ssd_mixer(proj, dt_raw, conv_w, conv_b, _pad_lanes(dt_bias), _pad_lanes(a_log), d_exp,
                      gnorm_w, batch, seq, d_inner)


def kernel(x, p, norm_w, ssd_in_w, ssd_conv_w, ssd_conv_b, ssd_dt_bias, ssd_a_log, ssd_d, ssd_gnorm_w, ssd_out_w, sb_in_w, sb_qn_w, sb_kn_w, sb_out_w, ple_norm_w, ple_gate_w, ple_proj_w):
    batch, seq, d_model = x.shape
    depth = p.shape[0]
    d_inner = ssd_out_w.shape[1]
    main = 2 * d_inner + 2 * SSD_N_GROUPS * SSD_D_STATE
    n_ssd_heads = ssd_in_w.shape[2] - main
    ssd_in_b = ssd_in_w.astype(BF16)
    ssd_dt_b = jnp.pad(ssd_in_w[:, :, main:], ((0, 0), (0, 0), (0, LANES - n_ssd_heads))).astype(BF16)
    ssd_out_b = ssd_out_w.astype(BF16)
    sb_in_b = sb_in_w.astype(BF16)
    sb_out_b = sb_out_w.astype(BF16)
    gate_b = ple_gate_w.astype(BF16)
    proj_b = ple_proj_w.astype(BF16)
    p_all = p.reshape(depth * batch * seq, p.shape[-1])
    h = x.reshape(batch * seq, d_model)
    for i in range(depth):
        j = i // 2
        if i % 2 == 0:
            a = _ssd_mix(h, norm_w[i], ssd_in_b, ssd_dt_b, j, ssd_conv_w[j], ssd_conv_b[j],
                         ssd_dt_bias[j], ssd_a_log[j], ssd_d[j], ssd_gnorm_w[j], d_inner, batch, seq)
            wo = ssd_out_b
        else:
            qkvg = _sb_in_proj(h, norm_w[i], sb_in_b, j, sb_qn_w[j], sb_kn_w[j])
            a = _sb_attention(qkvg, batch, seq)
            wo = sb_out_b
        h = _out_ple(a, wo, j, h, ple_norm_w[i], gate_b, p_all, proj_b, i)
    return h.reshape(batch, seq, d_model)
```

```python
import functools
import math

import jax
import jax.numpy as jnp
from jax import lax
from jax.experimental import pallas as pl
from jax.experimental.pallas import tpu as pltpu

F32 = jnp.float32
BF16 = jnp.bfloat16

NORM_EPS = 1e-6
GATED_NORM_EPS = 1e-5
LOG2E = 1.4426950408889634
SIGN_BIT = 0x80000000

SSD_HEAD_DIM = 64
SSD_N_GROUPS = 8
SSD_D_STATE = 128
SSD_D_CONV = 4
SSD_CHUNK = 128
SB_HEAD_DIM = 128

LANES = 128
SUBLANES = 8
BF16_SUBLANES = 16
V7X_VMEM_BYTES = 64 * 1024 * 1024
COMPILER_SCRATCH_BYTES = 12 * 1024 * 1024


def _vmem_limit(block_bytes, scratch_bytes=0):
    need = 2 * sum(block_bytes) + scratch_bytes + COMPILER_SCRATCH_BYTES
    return int(min(need, V7X_VMEM_BYTES - 8 * 1024 * 1024))


def _nbytes(shape, dtype):
    return math.prod(shape) * jnp.dtype(dtype).itemsize


def _pick_tile(n, pref):
    t = min(n, pref)
    assert n % t == 0, (n, t)
    return t


def _sigmoid(x):
    return 1.0 / (1.0 + jnp.exp(-x))


def _softplus(x):
    return jnp.maximum(x, 0.0) + jnp.log(1.0 + jnp.exp(-jnp.abs(x)))


def _rms_normalize(x, w, eps):
    ms = jnp.mean(x * x, axis=-1, keepdims=True)
    return x * lax.rsqrt(ms + eps) * w


def _row_block_ahead(i, done_with_block, n_blocks):
    return jnp.where(done_with_block, jnp.minimum(i + 1, n_blocks - 1), i)


STRIP_ROWS = 256


def _row_strips(rows):
    step = min(rows, STRIP_ROWS)
    return [slice(r, r + step) for r in range(0, rows, step)]


def _ssd_in_proj_kernel(x_ref, nw_ref, w_ref, wdt_ref, o_ref, dt_ref, xn_ref):
    @pl.when(pl.program_id(1) == 0)
    def _():
        xn_ref[...] = _rms_normalize(x_ref[...], nw_ref[...], NORM_EPS).astype(xn_ref.dtype)
        dt_ref[...] = jnp.dot(xn_ref[...], wdt_ref[...], preferred_element_type=F32)

    o_ref[...] = jnp.dot(xn_ref[...], w_ref[...], preferred_element_type=F32).astype(o_ref.dtype)


def _ssd_in_proj(x, nw, w, w_dt, layer, n, tm_pref=1024, tn_pref=1024):
    t, d = x.shape
    tm, tn = _pick_tile(t, tm_pref), _pick_tile(n, tn_pref)
    blocks = [_nbytes((tm, d), F32), _nbytes((d, tn), BF16), _nbytes((d, LANES), BF16),
              _nbytes((tm, tn), BF16), _nbytes((tm, LANES), F32)]
    return pl.pallas_call(
        _ssd_in_proj_kernel,
        grid=(t // tm, n // tn),
        in_specs=[
            pl.BlockSpec((tm, d), lambda i, j: (_row_block_ahead(i, j >= 1, t // tm), 0)),
            pl.BlockSpec((1, d), lambda i, j: (0, 0)),
            pl.BlockSpec((None, d, tn), lambda i, j: (layer, 0, j)),
            pl.BlockSpec((None, d, LANES), lambda i, j: (layer, 0, 0)),
        ],
        out_specs=[pl.BlockSpec((tm, tn), lambda i, j: (i, j)),
                   pl.BlockSpec((tm, LANES), lambda i, j: (i, 0))],
        out_shape=[jax.ShapeDtypeStruct((t, n), BF16), jax.ShapeDtypeStruct((t, LANES), F32)],
        scratch_shapes=[pltpu.VMEM((tm, d), BF16)],
        compiler_params=pltpu.CompilerParams(
            dimension_semantics=("parallel", "arbitrary"),
            vmem_limit_bytes=_vmem_limit(blocks, _nbytes((tm, d), BF16))),
        name="ssd_in_proj",
    )(x, nw.reshape(1, d), w, w_dt)


def _sb_in_proj_kernel(x_ref, nw_ref, w_ref, qn_ref, kn_ref, o_ref, xn_ref, *, tiles_per_section):
    j = pl.program_id(1)

    @pl.when(j == 0)
    def _():
        xn_ref[...] = _rms_normalize(x_ref[...], nw_ref[...], NORM_EPS).astype(xn_ref.dtype)

    section = j // tiles_per_section
    heads_per_tile = o_ref.shape[1] // SB_HEAD_DIM

    def project(epilogue):
        for rows in _row_strips(o_ref.shape[0]):
            acc = jnp.dot(xn_ref[rows, :], w_ref[...], preferred_element_type=F32)
            o_ref[rows, :] = epilogue(acc).astype(o_ref.dtype)

    def head_norm(hw_ref, scale):
        def epilogue(acc):
            heads = [_rms_normalize(acc[:, hh * SB_HEAD_DIM:(hh + 1) * SB_HEAD_DIM], hw_ref[...],
                                    NORM_EPS) * scale for hh in range(heads_per_tile)]
            return jnp.concatenate(heads, axis=1)
        return epilogue

    @pl.when(section == 0)
    def _():
        project(head_norm(qn_ref, LOG2E / math.sqrt(SB_HEAD_DIM)))

    @pl.when(section == 1)
    def _():
        project(head_norm(kn_ref, 1.0))

    @pl.when(section == 2)
    def _():
        project(lambda acc: acc)

    @pl.when(section == 3)
    def _():
        project(lambda acc: acc * _sigmoid(acc))


def _sb_in_proj(x, nw, w, layer, qn_w, kn_w, tm_pref=1024, tn_pref=1024):
    t, d = x.shape
    n = w.shape[2]
    width = n // 4
    tm, tn = _pick_tile(t, tm_pref), _pick_tile(width, tn_pref)
    blocks = [_nbytes((tm, d), F32), _nbytes((d, tn), BF16), _nbytes((tm, tn), BF16)]
    return pl.pallas_call(
        functools.partial(_sb_in_proj_kernel, tiles_per_section=width // tn),
        grid=(t // tm, n // tn),
        in_specs=[
            pl.BlockSpec((tm, d), lambda i, j: (_row_block_ahead(i, j >= 1, t // tm), 0)),
            pl.BlockSpec((1, d), lambda i, j: (0, 0)),
            pl.BlockSpec((None, d, tn), lambda i, j: (layer, 0, j)),
            pl.BlockSpec((1, SB_HEAD_DIM), lambda i, j: (0, 0)),
            pl.BlockSpec((1, SB_HEAD_DIM), lambda i, j: (0, 0)),
        ],
        out_specs=pl.BlockSpec((tm, tn), lambda i, j: (i, j)),
        out_shape=jax.ShapeDtypeStruct((t, n), BF16),
        scratch_shapes=[pltpu.VMEM((tm, d), BF16)],
        compiler_params=pltpu.CompilerParams(
            dimension_semantics=("parallel", "arbitrary"),
            vmem_limit_bytes=_vmem_limit(blocks, _nbytes((tm, d), BF16))),
        name="sb_in_proj",
    )(x, nw.reshape(1, d), w, qn_w.reshape(1, SB_HEAD_DIM), kn_w.reshape(1, SB_HEAD_DIM))


def _out_ple_kernel(a_ref, wo_ref, h_ref, nw_ref, gw_ref, p_ref, pw_ref, o_ref, h1_ref, xn_ref, *,
                    n_tiles):
    j = pl.program_id(1)
    tn = o_ref.shape[1]

    strips = _row_strips(o_ref.shape[0])

    @pl.when(j < n_tiles)
    def _():
        cols = pl.ds(pl.multiple_of(j * tn, tn), tn)
        for rows in strips:
            h1_ref[rows, cols] = h_ref[rows, :] + jnp.dot(a_ref[rows, :], wo_ref[...],
                                                          preferred_element_type=F32)

    @pl.when(j == n_tiles)
    def _():
        xn_ref[...] = _rms_normalize(h1_ref[...], nw_ref[...], NORM_EPS).astype(xn_ref.dtype)

    @pl.when(j >= n_tiles)
    def _():
        cols = pl.ds(pl.multiple_of((j - n_tiles) * tn, tn), tn)
        for rows in strips:
            gate = _sigmoid(jnp.dot(xn_ref[rows, :], gw_ref[...], preferred_element_type=F32))
            emb = jnp.dot(p_ref[rows, :].astype(BF16), pw_ref[...], preferred_element_type=F32)
            o_ref[rows, :] = h1_ref[rows, cols] + emb * gate


def _out_ple(a, wo, mixer_layer, h, nw, gw, p_all, pw, layer, tm_pref=1024, tn_pref=512):
    t, k = a.shape
    d = h.shape[1]
    pd = p_all.shape[1]
    tm, tn = _pick_tile(t, tm_pref), _pick_tile(d, tn_pref)
    nt = d // tn
    p_row0 = layer * (t // tm)
    first = lambda j: jnp.minimum(j, nt - 1)
    second = lambda j: jnp.maximum(j - nt, 0)
    blocks = [_nbytes((tm, k), BF16), _nbytes((k, tn), BF16), _nbytes((tm, tn), F32),
              _nbytes((d, tn), BF16), _nbytes((tm, pd), F32), _nbytes((pd, tn), BF16),
              _nbytes((tm, tn), F32)]
    scratch = _nbytes((tm, d), F32) + _nbytes((tm, d), BF16)
    return pl.pallas_call(
        functools.partial(_out_ple_kernel, n_tiles=nt),
        grid=(t // tm, 2 * nt),
        in_specs=[
            pl.BlockSpec((tm, k), lambda i, j: (_row_block_ahead(i, j >= nt, t // tm), 0)),
            pl.BlockSpec((None, k, tn), lambda i, j: (mixer_layer, 0, first(j))),
            pl.BlockSpec((tm, tn), lambda i, j: (i, first(j))),
            pl.BlockSpec((1, d), lambda i, j: (0, 0)),
            pl.BlockSpec((None, d, tn), lambda i, j: (layer, 0, second(j))),
            pl.BlockSpec((tm, pd), lambda i, j: (p_row0 + i, 0)),
            pl.BlockSpec((None, pd, tn), lambda i, j: (layer, 0, second(j))),
        ],
        out_specs=pl.BlockSpec((tm, tn), lambda i, j: (i, second(j))),
        out_shape=jax.ShapeDtypeStruct((t, d), F32),
        scratch_shapes=[pltpu.VMEM((tm, d), F32), pltpu.VMEM((tm, d), BF16)],
        compiler_params=pltpu.CompilerParams(
            dimension_semantics=("parallel", "arbitrary"),
            vmem_limit_bytes=_vmem_limit(blocks, scratch)),
        name="out_ple",
    )(a, wo, h, nw.reshape(1, d), gw, p_all, pw)


def _sb_attn_kernel(q_ref, k_ref, v_ref, g_ref, o_ref, *, blk, heads_per_step, strips):
    i = pl.program_id(2)
    hd = SB_HEAD_DIM
    row = lax.broadcasted_iota(jnp.int32, (blk, blk), 0)
    col = lax.broadcasted_iota(jnp.int32, (blk, blk), 1)
    strict = col < row
    later = (row > col).astype(BF16)
    strict_b = col.astype(F32).astype(BF16) < row.astype(F32).astype(BF16)

    rows = blk // strips
    units = [(slice(hh * hd, (hh + 1) * hd), slice(st * rows, (st + 1) * rows))
             for hh in range(heads_per_step) for st in range(strips)]

    def block(j, state, diagonal):
        start = pl.multiple_of(j * blk, blk)
        zs = [lax.dot_general(q_ref[rs, hs], k_ref[pl.ds(start, blk), hs], (((1,), (1,)), ((), ())),
                              preferred_element_type=F32) for hs, rs in units]
        logits, carries = [], []
        for u, z32 in enumerate(zs):
            z = z32.astype(BF16)
            log_beta = jnp.minimum(z, 0.0) - jnp.log(1.0 + jnp.exp2(-jnp.abs(z))) * LOG2E
            l1m = log_beta - z
            if diagonal:
                l1m = jnp.where(strict_b[units[u][1], :], l1m, 0.0)
            rest = jnp.dot(l1m, later, preferred_element_type=F32)
            logits.append(log_beta.astype(F32) + rest + state[u][0])
            carries.append(state[u][0] + rest[:, 0:1] + l1m[:, 0:1].astype(F32))
        out = []
        for u, logit in enumerate(logits):
            att = jnp.exp2(logit)
            if diagonal:
                att = jnp.where(strict[units[u][1], :], att, 0.0)
            acc = state[u][1] + jnp.dot(att.astype(BF16), v_ref[pl.ds(start, blk), units[u][0]],
                                        preferred_element_type=F32)
            out.append((carries[u], acc))
        return tuple(out)

    init = tuple((jnp.zeros((rows, 1), F32), jnp.zeros((rows, hd), F32)) for _ in units)
    state = block(i, init, True)
    state = lax.fori_loop(0, i, lambda t, s: block(i - 1 - t, s, False), state)
    for u, (hs, rs) in enumerate(units):
        o_ref[rs, hs] = (state[u][1] * g_ref[rs, hs].astype(F32)).astype(o_ref.dtype)


def _sb_attention(qkvg, batch, seq, blk_pref=256, heads_per_step=8, strips=1):
    t, n = qkvg.shape
    heads = n // (4 * SB_HEAD_DIM)
    blk = _pick_tile(seq, blk_pref)
    nq = seq // blk
    assert heads % heads_per_step == 0
    hsteps = heads // heads_per_step
    w = heads_per_step * SB_HEAD_DIM
    blocks = [_nbytes((blk, w), BF16), 2 * _nbytes((seq, w), BF16), 2 * _nbytes((blk, w), BF16)]
    return pl.pallas_call(
        functools.partial(_sb_attn_kernel, blk=blk, heads_per_step=heads_per_step, strips=strips),
        grid=(batch, hsteps, nq),
        in_specs=[
            pl.BlockSpec((blk, w), lambda b, h, i: (b * nq + i, h)),
            pl.BlockSpec((seq, w), lambda b, h, i: (b, hsteps + h)),
            pl.BlockSpec((seq, w), lambda b, h, i: (b, 2 * hsteps + h)),
            pl.BlockSpec((blk, w), lambda b, h, i: (b * nq + i, 3 * hsteps + h)),
        ],
        out_specs=pl.BlockSpec((blk, w), lambda b, h, i: (b * nq + i, h)),
        out_shape=jax.ShapeDtypeStruct((t, heads * SB_HEAD_DIM), BF16),
        compiler_params=pltpu.CompilerParams(
            dimension_semantics=("parallel", "parallel", "arbitrary"),
            vmem_limit_bytes=_vmem_limit(blocks)),
        name="sb_attention",
    )(qkvg, qkvg, qkvg, qkvg)


def _split3(v):
    hi = v.astype(BF16)
    r = v - hi.astype(F32)
    mid = r.astype(BF16)
    lo = (r - mid.astype(F32)).astype(BF16)
    return hi, mid, lo


def _ssd_kernel(z_ref, x_ref, b_ref, c_ref, dt_ref, spread_ref, shift_ref, convw_ref, convb_ref,
                dtb_ref, alog_ref, dexp_ref, gnw_ref, o_ref,
                tail_ref, state_ref, acum_sp_ref, ea_sp_ref, rowb_ref, wt_ref, *, d_inner):
    chunk = SSD_CHUNK
    heads_per_group = d_inner // SSD_N_GROUPS // SSD_HEAD_DIM
    gw = d_inner // SSD_N_GROUPS
    nbc = SSD_N_GROUPS * SSD_D_STATE
    first = pl.program_id(1) == 0

    @pl.when(first)
    def _():
        tail_ref[...] = jnp.zeros(tail_ref.shape, tail_ref.dtype)
        state_ref[...] = jnp.zeros(state_ref.shape, F32)

    dt = _softplus(dt_ref[...] + dtb_ref[...])
    adt = dt * (-LOG2E * jnp.exp(alog_ref[...]))
    li = lax.broadcasted_iota(jnp.int32, (chunk, chunk), 0)
    si = lax.broadcasted_iota(jnp.int32, (chunk, chunk), 1)
    causal = si <= li
    tri = causal.astype(BF16)
    acum = sum(jnp.dot(tri, piece, preferred_element_type=F32) for piece in _split3(adt))
    acum_t = acum.T
    dt_t = dt.T
    rowb_ref[...] = acum_t - jnp.log(dt_t) * LOG2E
    wt_ref[...] = dt_t * jnp.exp2(acum_t[:, chunk - 1:chunk] - acum_t)
    acum_sp_ref[...] = sum(jnp.dot(piece, spread_ref[...], preferred_element_type=F32)
                           for piece in _split3(acum))
    ea_sp_ref[...] = sum(jnp.dot(piece, spread_ref[...], preferred_element_type=F32)
                         for piece in _split3(jnp.exp2(acum)))

    lane_lo = lax.broadcasted_iota(jnp.int32, (chunk, LANES), 1) < SSD_HEAD_DIM

    def conv_silu(g):
        xcol = g * gw
        ncol = g * SSD_D_STATE
        srcs = ((x_ref, xcol, xcol, gw), (b_ref, ncol, d_inner + ncol, SSD_D_STATE),
                (c_ref, ncol, d_inner + nbc + ncol, SSD_D_STATE))
        raw = jnp.concatenate(
            [jnp.concatenate([tail_ref[:, pl.ds(tcol, width)], ref[:, pl.ds(col, width)]], axis=0)
             for ref, col, tcol, width in srcs], axis=1)
        w = jnp.concatenate([convw_ref[:, pl.ds(tcol, width)] for _, _, tcol, width in srcs], axis=1)
        acc = jnp.concatenate([convb_ref[:, pl.ds(tcol, width)] for _, _, tcol, width in srcs], axis=1)
        shifted = jnp.dot(shift_ref[...], raw, preferred_element_type=F32)
        for tap in range(SSD_D_CONV):
            k = SSD_D_CONV - 1 - tap
            acc = acc + w[tap:tap + 1, :] * shifted[k * chunk:(k + 1) * chunk, :]
        return acc * _sigmoid(acc)

    def scan_group(g, act):
        xcol = g * gw
        ncol = g * SSD_D_STATE
        xs = act[:, 0:gw]
        bg = act[:, gw:gw + SSD_D_STATE]
        cg = act[:, gw + SSD_D_STATE:gw + 2 * SSD_D_STATE]
        scores = lax.dot_general(cg.astype(BF16), bg.astype(BF16), (((1,), (1,)), ((), ())),
                                 preferred_element_type=F32)
        bg_t = bg.T
        hrow = g * SUBLANES
        acum_g = acum_sp_ref[:, pl.ds(ncol, LANES)]
        ea_g = ea_sp_ref[:, pl.ds(ncol, LANES)]
        rowb_g = rowb_ref[pl.ds(hrow, heads_per_group), :]
        wt_g = wt_ref[pl.ds(hrow, heads_per_group), :]
        st = state_ref[g]
        y_off = jnp.dot(cg.astype(BF16), st.astype(BF16), preferred_element_type=F32)

        pieces = []
        for pr in range(heads_per_group // 2):
            lhs_diag, lhs_state, ea_cols = [], [], []
            for r in (2 * pr, 2 * pr + 1):
                colb = jnp.broadcast_to(acum_g[:, r:r + 1], (chunk, chunk))
                decay_dt = jnp.where(causal, jnp.exp2(colb - rowb_g[r:r + 1, :]), 0.0)
                lhs_diag.append((decay_dt * scores).astype(BF16))
                lhs_state.append((bg_t * wt_g[r:r + 1, :]).astype(BF16))
                ea_cols.append(jnp.broadcast_to(ea_g[:, r:r + 1], (chunk, LANES)))
            psl = slice(pr * LANES, (pr + 1) * LANES)
            xp = xs[:, psl]
            x_bd = jnp.concatenate([jnp.where(lane_lo, xp, 0.0).astype(BF16),
                                    jnp.where(lane_lo, 0.0, xp).astype(BF16)], axis=0)
            ea_pair = jnp.where(lane_lo, ea_cols[0], ea_cols[1])
            y_pair = (jnp.dot(jnp.concatenate(lhs_diag, axis=1), x_bd, preferred_element_type=F32)
                      + ea_pair * y_off[:, psl])
            contrib = jnp.dot(jnp.concatenate(lhs_state, axis=1), x_bd, preferred_element_type=F32)
            state_ref[g, :, psl] = st[:, psl] * ea_pair[chunk - 1:chunk, :] + contrib
            pieces.append(y_pair)

        xsl = pl.ds(xcol, gw)
        y = jnp.concatenate(pieces, axis=1) + dexp_ref[:, xsl] * xs
        zt = z_ref[:, xsl].astype(F32)
        y = y * (zt * _sigmoid(zt))
        o_ref[:, xsl] = _rms_normalize(y, gnw_ref[:, xsl], GATED_NORM_EPS).astype(o_ref.dtype)

    act = conv_silu(0)
    for g in range(SSD_N_GROUPS):
        act_next = conv_silu(g + 1) if g + 1 < SSD_N_GROUPS else None
        scan_group(g, act)
        act = act_next

    keep = slice(chunk - BF16_SUBLANES, chunk)
    tail_ref[keep, 0:d_inner] = x_ref[keep, :]
    tail_ref[keep, d_inner:d_inner + nbc] = b_ref[keep, :]
    tail_ref[keep, d_inner + nbc:d_inner + 2 * nbc] = c_ref[keep, :]


def _ssd_mixer(proj, dt_raw, conv_w, conv_b, dt_bias_pad, a_log_pad, d_exp, gnorm_w, batch, seq,
               d_inner):
    t = proj.shape[0]
    chunk = SSD_CHUNK
    nc = seq // chunk
    nbc = SSD_N_GROUPS * SSD_D_STATE
    conv_dim = d_inner + 2 * nbc
    gw = d_inner // SSD_N_GROUPS
    heads_per_group = gw // SSD_HEAD_DIM
    assert heads_per_group == SUBLANES and d_inner % nbc == 0
    assert proj.dtype == BF16
    hh = jnp.arange(LANES)[:, None]
    cc = jnp.arange(SSD_N_GROUPS * LANES)[None, :]
    spread = ((cc % LANES < heads_per_group)
              & (hh == (cc // LANES) * heads_per_group + cc % LANES)).astype(BF16)
    rr = jnp.arange(SSD_D_CONV * chunk)[:, None]
    shift = (jnp.arange(2 * chunk)[None, :] == chunk + rr % chunk - rr // chunk).astype(BF16)
    row = lambda b, c: b * nc + c
    blocks = [2 * _nbytes((chunk, d_inner), proj.dtype), 2 * _nbytes((chunk, nbc), proj.dtype),
              _nbytes((chunk, LANES), F32), _nbytes(spread.shape, BF16), _nbytes(shift.shape, BF16),
              _nbytes((8, conv_dim), F32) * 2, _nbytes((chunk, d_inner), BF16)]
    scratch = (_nbytes((chunk, conv_dim), BF16) + _nbytes((SSD_N_GROUPS, SSD_D_STATE, gw), F32)
               + 2 * _nbytes((chunk, SSD_N_GROUPS * LANES), F32) + 2 * _nbytes((LANES, chunk), F32))
    full = lambda shape: pl.BlockSpec(shape, lambda b, c: (0, 0))
    return pl.pallas_call(
        functools.partial(_ssd_kernel, d_inner=d_inner),
        grid=(batch, nc),
        in_specs=[
            pl.BlockSpec((chunk, d_inner), lambda b, c: (row(b, c), 0)),
            pl.BlockSpec((chunk, d_inner), lambda b, c: (row(b, c), 1)),
            pl.BlockSpec((chunk, nbc), lambda b, c: (row(b, c), 2 * d_inner // nbc)),
            pl.BlockSpec((chunk, nbc), lambda b, c: (row(b, c), 2 * d_inner // nbc + 1)),
            pl.BlockSpec((chunk, LANES), lambda b, c: (row(b, c), 0)),
            full(spread.shape),
            full(shift.shape),
            full((SSD_D_CONV, conv_dim)),
            full((1, conv_dim)),
            full((1, LANES)),
            full((1, LANES)),
            full((1, d_inner)),
            full((1, d_inner)),
        ],
        out_specs=pl.BlockSpec((chunk, d_inner), lambda b, c: (row(b, c), 0)),
        out_shape=jax.ShapeDtypeStruct((t, d_inner), BF16),
        scratch_shapes=[
            pltpu.VMEM((chunk, conv_dim), proj.dtype),
            pltpu.VMEM((SSD_N_GROUPS, SSD_D_STATE, gw), F32),
            pltpu.VMEM((chunk, SSD_N_GROUPS * LANES), F32),
            pltpu.VMEM((chunk, SSD_N_GROUPS * LANES), F32),
            pltpu.VMEM((LANES, chunk), F32),
            pltpu.VMEM((LANES, chunk), F32),
        ],
        compiler_params=pltpu.CompilerParams(
            dimension_semantics=("parallel", "arbitrary"),
            vmem_limit_bytes=_vmem_limit(blocks, scratch)),
        name="ssd_mixer",
    )(proj, proj, proj, proj, dt_raw, spread, shift, conv_w, conv_b.reshape(1, conv_dim),
      dt_bias_pad, a_log_pad, d_exp, gnorm_w.reshape(1, d_inner))


def _pad_lanes(v):
    return jnp.pad(v.astype(F32), (0, LANES - v.shape[0])).reshape(1, LANES)


def _ssd_mix(h, nw, in_w, dt_w, layer, conv_w, conv_b, dt_bias, a_log, d_skip, gnorm_w, d_inner,
             batch, seq):
    main = 2 * d_inner + 2 * SSD_N_GROUPS * SSD_D_STATE
    proj, dt_raw = _ssd_in_proj(h, nw, in_w, dt_w, layer, main)
    d_exp = jnp.repeat(d_skip.astype(F32), SSD_HEAD_DIM).reshape(1, d_inner)
    return _ssd_mixer(proj, dt_raw, conv_w, conv_b, _pad_lanes(dt_bias), _pad_lanes(a_log), d_exp,
                      gnorm_w, batch, seq, d_inner)


def kernel(x, p, norm_w, ssd_in_w, ssd_conv_w, ssd_conv_b, ssd_dt_bias, ssd_a_log, ssd_d, ssd_gnorm_w, ssd_out_w, sb_in_w, sb_qn_w, sb_kn_w, sb_out_w, ple_norm_w, ple_gate_w, ple_proj_w):
    batch, seq, d_model = x.shape
    depth = p.shape[0]
    d_inner = ssd_out_w.shape[1]
    main = 2 * d_inner + 2 * SSD_N_GROUPS * SSD_D_STATE
    n_ssd_heads = ssd_in_w.shape[2] - main
    ssd_in_b = ssd_in_w.astype(BF16)
    ssd_dt_b = jnp.pad(ssd_in_w[:, :, main:], ((0, 0), (0, 0), (0, LANES - n_ssd_heads))).astype(BF16)
    ssd_out_b = ssd_out_w.astype(BF16)
    sb_in_b = sb_in_w.astype(BF16)
    sb_out_b = sb_out_w.astype(BF16)
    gate_b = ple_gate_w.astype(BF16)
    proj_b = ple_proj_w.astype(BF16)
    p_all = p.reshape(depth * batch * seq, p.shape[-1])
    h = x.reshape(batch * seq, d_model)
    for i in range(depth):
        j = i // 2
        if i % 2 == 0:
            a = _ssd_mix(h, norm_w[i], ssd_in_b, ssd_dt_b, j, ssd_conv_w[j], ssd_conv_b[j],
                         ssd_dt_bias[j], ssd_a_log[j], ssd_d[j], ssd_gnorm_w[j], d_inner, batch, seq)
            wo = ssd_out_b
        else:
            qkvg = _sb_in_proj(h, norm_w[i], sb_in_b, j, sb_qn_w[j], sb_kn_w[j])
            a = _sb_attention(qkvg, batch, seq)
            wo = sb_out_b
        h = _out_ple(a, wo, j, h, ple_norm_w[i], gate_b, p_all, proj_b, i)
    return h.reshape(batch, seq, d_model)
```

```python
import functools
import math

import jax
import jax.numpy as jnp
from jax import lax
from jax.experimental import pallas as pl
from jax.experimental.pallas import tpu as pltpu

F32 = jnp.float32
BF16 = jnp.bfloat16

NORM_EPS = 1e-6
GATED_NORM_EPS = 1e-5
LOG2E = 1.4426950408889634
SIGN_BIT = 0x80000000

SSD_HEAD_DIM = 64
SSD_N_GROUPS = 8
SSD_D_STATE = 128
SSD_D_CONV = 4
SSD_CHUNK = 128
SB_HEAD_DIM = 128

LANES = 128
SUBLANES = 8
BF16_SUBLANES = 16
V7X_VMEM_BYTES = 64 * 1024 * 1024
COMPILER_SCRATCH_BYTES = 8 * 1024 * 1024
VMEM_RESERVE_BYTES = 8 * 1024 * 1024


def _vmem_limit(block_bytes, scratch_bytes=0):
    need = 2 * sum(block_bytes) + scratch_bytes + COMPILER_SCRATCH_BYTES
    return int(min(need, V7X_VMEM_BYTES - VMEM_RESERVE_BYTES))


def _nbytes(shape, dtype):
    return math.prod(shape) * jnp.dtype(dtype).itemsize


def _pick_tile(n, pref):
    t = min(n, pref)
    assert n % t == 0, (n, t)
    return t


def _sigmoid(x):
    return 1.0 / (1.0 + jnp.exp(-x))


def _softplus(x):
    return jnp.maximum(x, 0.0) + jnp.log(1.0 + jnp.exp(-jnp.abs(x)))


def _rms_normalize(x, w, eps):
    ms = jnp.mean(x * x, axis=-1, keepdims=True)
    return x * lax.rsqrt(ms + eps) * w


def _row_block_ahead(i, done_with_block, n_blocks):
    return jnp.where(done_with_block, jnp.minimum(i + 1, n_blocks - 1), i)


STRIP_ROWS = 256
ATTN_WAVES = 2


def _row_strips(rows):
    step = min(rows, STRIP_ROWS)
    return [slice(r, r + step) for r in range(0, rows, step)]


def _ssd_in_proj_kernel(x_ref, nw_ref, w_ref, wdt_ref, o_ref, dt_ref, xn_ref):
    @pl.when(pl.program_id(1) == 0)
    def _():
        xn_ref[...] = _rms_normalize(x_ref[...], nw_ref[...], NORM_EPS).astype(xn_ref.dtype)
        dt_ref[...] = jnp.dot(xn_ref[...], wdt_ref[...], preferred_element_type=F32)

    o_ref[...] = jnp.dot(xn_ref[...], w_ref[...], preferred_element_type=F32).astype(o_ref.dtype)


def _ssd_in_proj(x, nw, w, w_dt, layer, n, tm_pref=1024, tn_pref=1024):
    t, d = x.shape
    tm, tn = _pick_tile(t, tm_pref), _pick_tile(n, tn_pref)
    blocks = [_nbytes((tm, d), F32), _nbytes((d, tn), BF16), _nbytes((d, LANES), BF16),
              _nbytes((tm, tn), BF16), _nbytes((tm, LANES), F32)]
    return pl.pallas_call(
        _ssd_in_proj_kernel,
        grid=(t // tm, n // tn),
        in_specs=[
            pl.BlockSpec((tm, d), lambda i, j: (_row_block_ahead(i, j >= 1, t // tm), 0)),
            pl.BlockSpec((1, d), lambda i, j: (0, 0)),
            pl.BlockSpec((None, d, tn), lambda i, j: (layer, 0, j)),
            pl.BlockSpec((None, d, LANES), lambda i, j: (layer, 0, 0)),
        ],
        out_specs=[pl.BlockSpec((tm, tn), lambda i, j: (i, j)),
                   pl.BlockSpec((tm, LANES), lambda i, j: (i, 0))],
        out_shape=[jax.ShapeDtypeStruct((t, n), BF16), jax.ShapeDtypeStruct((t, LANES), F32)],
        scratch_shapes=[pltpu.VMEM((tm, d), BF16)],
        compiler_params=pltpu.CompilerParams(
            dimension_semantics=("parallel", "arbitrary"),
            vmem_limit_bytes=_vmem_limit(blocks, _nbytes((tm, d), BF16))),
        name="ssd_in_proj",
    )(x, nw.reshape(1, d), w, w_dt)


def _sb_in_proj_kernel(x_ref, nw_ref, w_ref, qn_ref, kn_ref, o_ref, xn_ref, *, tiles_per_section):
    j = pl.program_id(1)

    @pl.when(j == 0)
    def _():
        xn_ref[...] = _rms_normalize(x_ref[...], nw_ref[...], NORM_EPS).astype(xn_ref.dtype)

    section = j // tiles_per_section
    heads_per_tile = o_ref.shape[1] // SB_HEAD_DIM

    def project(epilogue):
        for rows in _row_strips(o_ref.shape[0]):
            acc = jnp.dot(xn_ref[rows, :], w_ref[...], preferred_element_type=F32)
            o_ref[rows, :] = epilogue(acc).astype(o_ref.dtype)

    def head_norm(hw_ref, scale):
        def epilogue(acc):
            heads = [_rms_normalize(acc[:, hh * SB_HEAD_DIM:(hh + 1) * SB_HEAD_DIM], hw_ref[...],
                                    NORM_EPS) * scale for hh in range(heads_per_tile)]
            return jnp.concatenate(heads, axis=1)
        return epilogue

    @pl.when(section == 0)
    def _():
        project(head_norm(qn_ref, LOG2E / math.sqrt(SB_HEAD_DIM)))

    @pl.when(section == 1)
    def _():
        project(head_norm(kn_ref, 1.0))

    @pl.when(section == 2)
    def _():
        project(lambda acc: acc)

    @pl.when(section == 3)
    def _():
        project(lambda acc: acc * _sigmoid(acc))


def _sb_in_proj(x, nw, w, layer, qn_w, kn_w, tm_pref=1024, tn_pref=1024):
    t, d = x.shape
    n = w.shape[2]
    width = n // 4
    tm, tn = _pick_tile(t, tm_pref), _pick_tile(width, tn_pref)
    blocks = [_nbytes((tm, d), F32), _nbytes((d, tn), BF16), _nbytes((tm, tn), BF16)]
    return pl.pallas_call(
        functools.partial(_sb_in_proj_kernel, tiles_per_section=width // tn),
        grid=(t // tm, n // tn),
        in_specs=[
            pl.BlockSpec((tm, d), lambda i, j: (_row_block_ahead(i, j >= 1, t // tm), 0)),
            pl.BlockSpec((1, d), lambda i, j: (0, 0)),
            pl.BlockSpec((None, d, tn), lambda i, j: (layer, 0, j)),
            pl.BlockSpec((1, SB_HEAD_DIM), lambda i, j: (0, 0)),
            pl.BlockSpec((1, SB_HEAD_DIM), lambda i, j: (0, 0)),
        ],
        out_specs=pl.BlockSpec((tm, tn), lambda i, j: (i, j)),
        out_shape=jax.ShapeDtypeStruct((t, n), BF16),
        scratch_shapes=[pltpu.VMEM((tm, d), BF16)],
        compiler_params=pltpu.CompilerParams(
            dimension_semantics=("parallel", "arbitrary"),
            vmem_limit_bytes=_vmem_limit(blocks, _nbytes((tm, d), BF16))),
        name="sb_in_proj",
    )(x, nw.reshape(1, d), w, qn_w.reshape(1, SB_HEAD_DIM), kn_w.reshape(1, SB_HEAD_DIM))


def _out_ple_kernel(a_ref, wo_ref, h_ref, nw_ref, gw_ref, p_ref, pw_ref, o_ref, h1_ref, xn_ref, *,
                    n_tiles):
    j = pl.program_id(1)
    tn = o_ref.shape[1]

    strips = _row_strips(o_ref.shape[0])

    def tile(w_ref, cols):
        return w_ref[...] if w_ref.shape[1] == tn else w_ref[:, cols]

    @pl.when(j < n_tiles)
    def _():
        cols = pl.ds(pl.multiple_of(j * tn, tn), tn)
        wo = tile(wo_ref, cols)
        for rows in strips:
            h1_ref[rows, cols] = h_ref[rows, :] + jnp.dot(a_ref[rows, :], wo,
                                                          preferred_element_type=F32)

    @pl.when(j == n_tiles)
    def _():
        xn_ref[...] = _rms_normalize(h1_ref[...], nw_ref[...], NORM_EPS).astype(xn_ref.dtype)

    @pl.when(j >= n_tiles)
    def _():
        cols = pl.ds(pl.multiple_of((j - n_tiles) * tn, tn), tn)
        gw = tile(gw_ref, cols)
        for rows in strips:
            gate = _sigmoid(jnp.dot(xn_ref[rows, :], gw, preferred_element_type=F32))
            emb = jnp.dot(p_ref[rows, :].astype(BF16), pw_ref[...], preferred_element_type=F32)
            o_ref[rows, :] = h1_ref[rows, cols] + emb * gate


def _out_ple(a, wo, mixer_layer, h, nw, gw, p_all, pw, layer, tm_pref=1024, tn_pref=512):
    t, k = a.shape
    d = h.shape[1]
    pd = p_all.shape[1]
    tm, tn = _pick_tile(t, tm_pref), _pick_tile(d, tn_pref)
    nt = d // tn
    p_row0 = layer * (t // tm)
    first = lambda j: jnp.minimum(j, nt - 1)
    second = lambda j: jnp.maximum(j - nt, 0)
    per_step = [_nbytes((tm, k), BF16), _nbytes((tm, tn), F32), _nbytes((tm, pd), F32),
                _nbytes((pd, tn), BF16), _nbytes((tm, tn), F32)]
    scratch = _nbytes((tm, d), F32) + _nbytes((tm, d), BF16)
    whole = _nbytes((k, d), BF16) + _nbytes((d, d), BF16)
    resident = (2 * sum(per_step) + whole + scratch + COMPILER_SCRATCH_BYTES
                <= V7X_VMEM_BYTES - VMEM_RESERVE_BYTES)
    if resident:
        wo_spec = pl.BlockSpec((None, k, d), lambda i, j: (mixer_layer, 0, 0),
                               pipeline_mode=pl.Buffered(1))
        gw_spec = pl.BlockSpec((None, d, d), lambda i, j: (layer, 0, 0),
                               pipeline_mode=pl.Buffered(1))
        limit = _vmem_limit(per_step, scratch + whole)
    else:
        wo_spec = pl.BlockSpec((None, k, tn), lambda i, j: (mixer_layer, 0, first(j)))
        gw_spec = pl.BlockSpec((None, d, tn), lambda i, j: (layer, 0, second(j)))
        limit = _vmem_limit(per_step + [_nbytes((k, tn), BF16), _nbytes((d, tn), BF16)], scratch)
    return pl.pallas_call(
        functools.partial(_out_ple_kernel, n_tiles=nt),
        grid=(t // tm, 2 * nt),
        in_specs=[
            pl.BlockSpec((tm, k), lambda i, j: (_row_block_ahead(i, j >= nt, t // tm), 0)),
            wo_spec,
            pl.BlockSpec((tm, tn), lambda i, j: (i, first(j))),
            pl.BlockSpec((1, d), lambda i, j: (0, 0)),
            gw_spec,
            pl.BlockSpec((tm, pd), lambda i, j: (p_row0 + i, 0)),
            pl.BlockSpec((None, pd, tn), lambda i, j: (layer, 0, second(j))),
        ],
        out_specs=pl.BlockSpec((tm, tn), lambda i, j: (i, second(j))),
        out_shape=jax.ShapeDtypeStruct((t, d), F32),
        scratch_shapes=[pltpu.VMEM((tm, d), F32), pltpu.VMEM((tm, d), BF16)],
        compiler_params=pltpu.CompilerParams(
            dimension_semantics=("parallel", "arbitrary"),
            vmem_limit_bytes=limit),
        name="out_ple",
    )(a, wo, h, nw.reshape(1, d), gw, p_all, pw)


def _sb_attn_kernel(q_ref, k_ref, v_ref, g_ref, o_ref, *, blk, heads_per_step, strips):
    i = pl.program_id(2)
    hd = SB_HEAD_DIM
    row = lax.broadcasted_iota(jnp.int32, (blk, blk), 0)
    col = lax.broadcasted_iota(jnp.int32, (blk, blk), 1)
    strict = col < row
    later = (row > col).astype(BF16)
    strict_b = col.astype(F32).astype(BF16) < row.astype(F32).astype(BF16)

    rows = blk // strips
    units = [(slice(hh * hd, (hh + 1) * hd), slice(st * rows, (st + 1) * rows))
             for hh in range(heads_per_step) for st in range(strips)]

    def block(j, state, diagonal):
        start = pl.multiple_of(j * blk, blk)
        zs, logits, carries, out = {}, {}, {}, {}

        def scores(us):
            for u in us:
                hs, rs = units[u]
                zs[u] = lax.dot_general(q_ref[rs, hs], k_ref[pl.ds(start, blk), hs],
                                        (((1,), (1,)), ((), ())), preferred_element_type=F32)

        def log_terms(us):
            for u in us:
                z = zs[u].astype(BF16)
                log_beta = jnp.minimum(z, 0.0) - jnp.log(1.0 + jnp.exp2(-jnp.abs(z))) * LOG2E
                l1m = log_beta - z
                if diagonal:
                    l1m = jnp.where(strict_b[units[u][1], :], l1m, 0.0)
                rest = jnp.dot(l1m, later, preferred_element_type=F32)
                logits[u] = log_beta.astype(F32) + rest + state[u][0]
                carries[u] = state[u][0] + rest[:, 0:1] + l1m[:, 0:1].astype(F32)

        def weighted_values(us):
            for u in us:
                att = jnp.exp2(logits[u])
                if diagonal:
                    att = jnp.where(strict[units[u][1], :], att, 0.0)
                acc = state[u][1] + jnp.dot(att.astype(BF16), v_ref[pl.ds(start, blk), units[u][0]],
                                            preferred_element_type=F32)
                out[u] = (carries[u], acc)

        per_wave = len(units) // ATTN_WAVES
        waves = [list(range(w * per_wave, (w + 1) * per_wave)) for w in range(ATTN_WAVES)]
        scores(waves[0])
        for w in range(ATTN_WAVES):
            log_terms(waves[w])
            if w + 1 < ATTN_WAVES:
                scores(waves[w + 1])
            weighted_values(waves[w])
        return tuple(out[u] for u in range(len(units)))

    init = tuple((jnp.zeros((rows, 1), F32), jnp.zeros((rows, hd), F32)) for _ in units)
    state = block(i, init, True)
    state = lax.fori_loop(0, i, lambda t, s: block(i - 1 - t, s, False), state)
    for u, (hs, rs) in enumerate(units):
        o_ref[rs, hs] = (state[u][1] * g_ref[rs, hs].astype(F32)).astype(o_ref.dtype)


def _sb_attention(qkvg, batch, seq, blk_pref=256, heads_per_step=8, strips=1):
    t, n = qkvg.shape
    heads = n // (4 * SB_HEAD_DIM)
    blk = _pick_tile(seq, blk_pref)
    nq = seq // blk
    assert heads % heads_per_step == 0
    hsteps = heads // heads_per_step
    w = heads_per_step * SB_HEAD_DIM
    blocks = [_nbytes((blk, w), BF16), 2 * _nbytes((seq, w), BF16), 2 * _nbytes((blk, w), BF16)]
    return pl.pallas_call(
        functools.partial(_sb_attn_kernel, blk=blk, heads_per_step=heads_per_step, strips=strips),
        grid=(batch, hsteps, nq),
        in_specs=[
            pl.BlockSpec((blk, w), lambda b, h, i: (b * nq + i, h)),
            pl.BlockSpec((seq, w), lambda b, h, i: (b, hsteps + h)),
            pl.BlockSpec((seq, w), lambda b, h, i: (b, 2 * hsteps + h)),
            pl.BlockSpec((blk, w), lambda b, h, i: (b * nq + i, 3 * hsteps + h)),
        ],
        out_specs=pl.BlockSpec((blk, w), lambda b, h, i: (b * nq + i, h)),
        out_shape=jax.ShapeDtypeStruct((t, heads * SB_HEAD_DIM), BF16),
        compiler_params=pltpu.CompilerParams(
            dimension_semantics=("parallel", "parallel", "arbitrary"),
            vmem_limit_bytes=_vmem_limit(blocks)),
        name="sb_attention",
    )(qkvg, qkvg, qkvg, qkvg)


def _split3(v):
    hi = v.astype(BF16)
    r = v - hi.astype(F32)
    mid = r.astype(BF16)
    lo = (r - mid.astype(F32)).astype(BF16)
    return hi, mid, lo


def _ssd_kernel(z_ref, x_ref, b_ref, c_ref, dt_ref, spread_ref, shift_ref, convw_ref, convb_ref,
                dtb_ref, alog_ref, dexp_ref, gnw_ref, o_ref,
                tail_ref, state_ref, acum_sp_ref, ea_sp_ref, rowb_ref, wt_ref, *, d_inner):
    chunk = SSD_CHUNK
    heads_per_group = d_inner // SSD_N_GROUPS // SSD_HEAD_DIM
    gw = d_inner // SSD_N_GROUPS
    nbc = SSD_N_GROUPS * SSD_D_STATE
    first = pl.program_id(1) == 0

    @pl.when(first)
    def _():
        tail_ref[...] = jnp.zeros(tail_ref.shape, tail_ref.dtype)
        state_ref[...] = jnp.zeros(state_ref.shape, F32)

    dt = _softplus(dt_ref[...] + dtb_ref[...])
    adt = dt * (-LOG2E * jnp.exp(alog_ref[...]))
    li = lax.broadcasted_iota(jnp.int32, (chunk, chunk), 0)
    si = lax.broadcasted_iota(jnp.int32, (chunk, chunk), 1)
    causal = si <= li
    tri = causal.astype(BF16)
    acum = sum(jnp.dot(tri, piece, preferred_element_type=F32) for piece in _split3(adt))
    acum_t = acum.T
    dt_t = dt.T
    rowb_ref[...] = acum_t - jnp.log(dt_t) * LOG2E
    wt_ref[...] = dt_t * jnp.exp2(acum_t[:, chunk - 1:chunk] - acum_t)
    acum_sp_ref[...] = sum(jnp.dot(piece, spread_ref[...], preferred_element_type=F32)
                           for piece in _split3(acum))
    ea_sp_ref[...] = sum(jnp.dot(piece, spread_ref[...], preferred_element_type=F32)
                         for piece in _split3(jnp.exp2(acum)))

    lane_lo = lax.broadcasted_iota(jnp.int32, (chunk, LANES), 1) < SSD_HEAD_DIM

    def conv_silu(g):
        xcol = g * gw
        ncol = g * SSD_D_STATE
        srcs = ((x_ref, xcol, xcol, gw), (b_ref, ncol, d_inner + ncol, SSD_D_STATE),
                (c_ref, ncol, d_inner + nbc + ncol, SSD_D_STATE))
        raw = jnp.concatenate(
            [jnp.concatenate([tail_ref[:, pl.ds(tcol, width)], ref[:, pl.ds(col, width)]], axis=0)
             for ref, col, tcol, width in srcs], axis=1)
        w = jnp.concatenate([convw_ref[:, pl.ds(tcol, width)] for _, _, tcol, width in srcs], axis=1)
        acc = jnp.concatenate([convb_ref[:, pl.ds(tcol, width)] for _, _, tcol, width in srcs], axis=1)
        shifted = jnp.dot(shift_ref[...], raw, preferred_element_type=F32)
        for tap in range(SSD_D_CONV):
            k = SSD_D_CONV - 1 - tap
            acc = acc + w[tap:tap + 1, :] * shifted[k * chunk:(k + 1) * chunk, :]
        return acc * _sigmoid(acc)

    def scan_group(g, act):
        xcol = g * gw
        ncol = g * SSD_D_STATE
        xs = act[:, 0:gw]
        bg = act[:, gw:gw + SSD_D_STATE]
        cg = act[:, gw + SSD_D_STATE:gw + 2 * SSD_D_STATE]
        scores = lax.dot_general(cg.astype(BF16), bg.astype(BF16), (((1,), (1,)), ((), ())),
                                 preferred_element_type=F32)
        bg_t = bg.T
        hrow = g * SUBLANES
        acum_g = acum_sp_ref[:, pl.ds(ncol, LANES)]
        ea_g = ea_sp_ref[:, pl.ds(ncol, LANES)]
        rowb_g = rowb_ref[pl.ds(hrow, heads_per_group), :]
        wt_g = wt_ref[pl.ds(hrow, heads_per_group), :]
        cg_b = cg.astype(BF16)

        pieces = []
        for pr in range(heads_per_group // 2):
            lhs_diag, lhs_state, ea_cols = [], [], []
            for r in (2 * pr, 2 * pr + 1):
                colb = jnp.broadcast_to(acum_g[:, r:r + 1], (chunk, chunk))
                decay_dt = jnp.where(causal, jnp.exp2(colb - rowb_g[r:r + 1, :]), 0.0)
                lhs_diag.append((decay_dt * scores).astype(BF16))
                lhs_state.append((bg_t * wt_g[r:r + 1, :]).astype(BF16))
                ea_cols.append(jnp.broadcast_to(ea_g[:, r:r + 1], (chunk, LANES)))
            psl = slice(pr * LANES, (pr + 1) * LANES)
            xp = xs[:, psl]
            x_bd = jnp.concatenate([jnp.where(lane_lo, xp, 0.0).astype(BF16),
                                    jnp.where(lane_lo, 0.0, xp).astype(BF16)], axis=0)
            ea_pair = jnp.where(lane_lo, ea_cols[0], ea_cols[1])
            st = state_ref[g, :, psl]
            y_off = jnp.dot(cg_b, st.astype(BF16), preferred_element_type=F32)
            y_pair = (jnp.dot(jnp.concatenate(lhs_diag, axis=1), x_bd, preferred_element_type=F32)
                      + ea_pair * y_off)
            contrib = jnp.dot(jnp.concatenate(lhs_state, axis=1), x_bd, preferred_element_type=F32)
            state_ref[g, :, psl] = st * ea_pair[chunk - 1:chunk, :] + contrib
            pieces.append(y_pair)

        xsl = pl.ds(xcol, gw)
        y = jnp.concatenate(pieces, axis=1) + dexp_ref[:, xsl] * xs
        zt = z_ref[:, xsl].astype(F32)
        y = y * (zt * _sigmoid(zt))
        o_ref[:, xsl] = _rms_normalize(y, gnw_ref[:, xsl], GATED_NORM_EPS).astype(o_ref.dtype)

    act = conv_silu(0)
    for g in range(SSD_N_GROUPS):
        act_next = conv_silu(g + 1) if g + 1 < SSD_N_GROUPS else None
        scan_group(g, act)
        act = act_next

    keep = slice(chunk - BF16_SUBLANES, chunk)
    tail_ref[keep, 0:d_inner] = x_ref[keep, :]
    tail_ref[keep, d_inner:d_inner + nbc] = b_ref[keep, :]
    tail_ref[keep, d_inner + nbc:d_inner + 2 * nbc] = c_ref[keep, :]


def _ssd_mixer(proj, dt_raw, conv_w, conv_b, dt_bias_pad, a_log_pad, d_exp, gnorm_w, batch, seq,
               d_inner):
    t = proj.shape[0]
    chunk = SSD_CHUNK
    nc = seq // chunk
    nbc = SSD_N_GROUPS * SSD_D_STATE
    conv_dim = d_inner + 2 * nbc
    gw = d_inner // SSD_N_GROUPS
    heads_per_group = gw // SSD_HEAD_DIM
    assert heads_per_group == SUBLANES and d_inner % nbc == 0
    assert proj.dtype == BF16
    hh = jnp.arange(LANES)[:, None]
    cc = jnp.arange(SSD_N_GROUPS * LANES)[None, :]
    spread = ((cc % LANES < heads_per_group)
              & (hh == (cc // LANES) * heads_per_group + cc % LANES)).astype(BF16)
    rr = jnp.arange(SSD_D_CONV * chunk)[:, None]
    shift = (jnp.arange(2 * chunk)[None, :] == chunk + rr % chunk - rr // chunk).astype(BF16)
    row = lambda b, c: b * nc + c
    blocks = [2 * _nbytes((chunk, d_inner), proj.dtype), 2 * _nbytes((chunk, nbc), proj.dtype),
              _nbytes((chunk, LANES), F32), _nbytes(spread.shape, BF16), _nbytes(shift.shape, BF16),
              _nbytes((8, conv_dim), F32) * 2, _nbytes((chunk, d_inner), BF16)]
    scratch = (_nbytes((chunk, conv_dim), BF16) + _nbytes((SSD_N_GROUPS, SSD_D_STATE, gw), F32)
               + 2 * _nbytes((chunk, SSD_N_GROUPS * LANES), F32) + 2 * _nbytes((LANES, chunk), F32))
    full = lambda shape: pl.BlockSpec(shape, lambda b, c: (0, 0))
    return pl.pallas_call(
        functools.partial(_ssd_kernel, d_inner=d_inner),
        grid=(batch, nc),
        in_specs=[
            pl.BlockSpec((chunk, d_inner), lambda b, c: (row(b, c), 0)),
            pl.BlockSpec((chunk, d_inner), lambda b, c: (row(b, c), 1)),
            pl.BlockSpec((chunk, nbc), lambda b, c: (row(b, c), 2 * d_inner // nbc)),
            pl.BlockSpec((chunk, nbc), lambda b, c: (row(b, c), 2 * d_inner // nbc + 1)),
            pl.BlockSpec((chunk, LANES), lambda b, c: (row(b, c), 0)),
            full(spread.shape),
            full(shift.shape),
            full((SSD_D_CONV, conv_dim)),
            full((1, conv_dim)),
            full((1, LANES)),
            full((1, LANES)),
            full((1, d_inner)),
            full((1, d_inner)),
        ],
        out_specs=pl.BlockSpec((chunk, d_inner), lambda b, c: (row(b, c), 0)),
        out_shape=jax.ShapeDtypeStruct((t, d_inner), BF16),
        scratch_shapes=[
            pltpu.VMEM((chunk, conv_dim), proj.dtype),
            pltpu.VMEM((SSD_N_GROUPS, SSD_D_STATE, gw), F32),
            pltpu.VMEM((chunk, SSD_N_GROUPS * LANES), F32),
            pltpu.VMEM((chunk, SSD_N_GROUPS * LANES), F32),
            pltpu.VMEM((LANES, chunk), F32),
            pltpu.VMEM((LANES, chunk), F32),
        ],
        compiler_params=pltpu.CompilerParams(
            dimension_semantics=("parallel", "arbitrary"),
            vmem_limit_bytes=_vmem_limit(blocks, scratch)),
        name="ssd_mixer",
    )(proj, proj, proj, proj, dt_raw, spread, shift, conv_w, conv_b.reshape(1, conv_dim),
      dt_bias_pad, a_log_pad, d_exp, gnorm_w.reshape(1, d_inner))


def _pad_lanes(v):
    return jnp.pad(v.astype(F32), (0, LANES - v.shape[0])).reshape(1, LANES)


def _ssd_mix(h, nw, in_w, dt_w, layer, conv_w, conv_b, dt_bias, a_log, d_skip, gnorm_w, d_inner,
             batch, seq):
    main = 2 * d_inner + 2 * SSD_N_GROUPS * SSD_D_STATE
    proj, dt_raw = _ssd_in_proj(h, nw, in_w, dt_w, layer, main)
    d_exp = jnp.repeat(d_skip.astype(F32), SSD_HEAD_DIM).reshape(1, d_inner)
    return _ssd_mixer(proj, dt_raw, conv_w, conv_b, _pad_lanes(dt_bias), _pad_lanes(a_log), d_exp,
                      gnorm_w, batch, seq, d_inner)


def kernel(x, p, norm_w, ssd_in_w, ssd_conv_w, ssd_conv_b, ssd_dt_bias, ssd_a_log, ssd_d, ssd_gnorm_w, ssd_out_w, sb_in_w, sb_qn_w, sb_kn_w, sb_out_w, ple_norm_w, ple_gate_w, ple_proj_w):
    batch, seq, d_model = x.shape
    depth = p.shape[0]
    d_inner = ssd_out_w.shape[1]
    main = 2 * d_inner + 2 * SSD_N_GROUPS * SSD_D_STATE
    n_ssd_heads = ssd_in_w.shape[2] - main
    ssd_in_b = ssd_in_w.astype(BF16)
    ssd_dt_b = jnp.pad(ssd_in_w[:, :, main:], ((0, 0), (0, 0), (0, LANES - n_ssd_heads))).astype(BF16)
    ssd_out_b = ssd_out_w.astype(BF16)
    sb_in_b = sb_in_w.astype(BF16)
    sb_out_b = sb_out_w.astype(BF16)
    gate_b = ple_gate_w.astype(BF16)
    proj_b = ple_proj_w.astype(BF16)
    p_all = p.reshape(depth * batch * seq, p.shape[-1])
    h = x.reshape(batch * seq, d_model)
    for i in range(depth):
        j = i // 2
        if i % 2 == 0:
            a = _ssd_mix(h, norm_w[i], ssd_in_b, ssd_dt_b, j, ssd_conv_w[j], ssd_conv_b[j],
                         ssd_dt_bias[j], ssd_a_log[j], ssd_d[j], ssd_gnorm_w[j], d_inner, batch, seq)
            wo = ssd_out_b
        else:
            qkvg = _sb_in_proj(h, norm_w[i], sb_in_b, j, sb_qn_w[j], sb_kn_w[j])
            a = _sb_attention(qkvg, batch, seq)
            wo = sb_out_b
        h = _out_ple(a, wo, j, h, ple_norm_w[i], gate_b, p_all, proj_b, i)
    return h.reshape(batch, seq, d_model)
```

```python
import functools
import math

import jax
import jax.numpy as jnp
from jax import lax
from jax.experimental import pallas as pl
from jax.experimental.pallas import tpu as pltpu

F32 = jnp.float32
BF16 = jnp.bfloat16

NORM_EPS = 1e-6
GATED_NORM_EPS = 1e-5
LOG2E = 1.4426950408889634
SIGN_BIT = 0x80000000

SSD_HEAD_DIM = 64
SSD_N_GROUPS = 8
SSD_D_STATE = 128
SSD_D_CONV = 4
SSD_CHUNK = 128
SB_HEAD_DIM = 128

LANES = 128
SUBLANES = 8
BF16_SUBLANES = 16
V7X_VMEM_BYTES = 64 * 1024 * 1024
COMPILER_SCRATCH_BYTES = 8 * 1024 * 1024
VMEM_RESERVE_BYTES = 8 * 1024 * 1024


def _vmem_limit(block_bytes, scratch_bytes=0):
    need = 2 * sum(block_bytes) + scratch_bytes + COMPILER_SCRATCH_BYTES
    return int(min(need, V7X_VMEM_BYTES - VMEM_RESERVE_BYTES))


def _nbytes(shape, dtype):
    return math.prod(shape) * jnp.dtype(dtype).itemsize


def _pick_tile(n, pref):
    t = min(n, pref)
    assert n % t == 0, (n, t)
    return t


def _sigmoid(x):
    return 1.0 / (1.0 + jnp.exp(-x))


def _softplus(x):
    return jnp.maximum(x, 0.0) + jnp.log(1.0 + jnp.exp(-jnp.abs(x)))


def _rms_normalize(x, w, eps):
    ms = jnp.mean(x * x, axis=-1, keepdims=True)
    return x * lax.rsqrt(ms + eps) * w


def _row_block_ahead(i, done_with_block, n_blocks):
    return jnp.where(done_with_block, jnp.minimum(i + 1, n_blocks - 1), i)


STRIP_ROWS = 256
ATTN_WAVES = 2
ATTN_DEAD_LOG2 = -152.0


def _row_strips(rows):
    step = min(rows, STRIP_ROWS)
    return [slice(r, r + step) for r in range(0, rows, step)]


def _ssd_in_proj_kernel(x_ref, nw_ref, w_ref, wdt_ref, o_ref, dt_ref, xn_ref):
    @pl.when(pl.program_id(1) == 0)
    def _():
        xn_ref[...] = _rms_normalize(x_ref[...], nw_ref[...], NORM_EPS).astype(xn_ref.dtype)
        dt_ref[...] = jnp.dot(xn_ref[...], wdt_ref[...], preferred_element_type=F32)

    o_ref[...] = jnp.dot(xn_ref[...], w_ref[...], preferred_element_type=F32).astype(o_ref.dtype)


def _ssd_in_proj(x, nw, w, w_dt, layer, n, tm_pref=1024, tn_pref=1024):
    t, d = x.shape
    tm, tn = _pick_tile(t, tm_pref), _pick_tile(n, tn_pref)
    blocks = [_nbytes((tm, d), F32), _nbytes((d, tn), BF16), _nbytes((d, LANES), BF16),
              _nbytes((tm, tn), BF16), _nbytes((tm, LANES), F32)]
    return pl.pallas_call(
        _ssd_in_proj_kernel,
        grid=(t // tm, n // tn),
        in_specs=[
            pl.BlockSpec((tm, d), lambda i, j: (_row_block_ahead(i, j >= 1, t // tm), 0)),
            pl.BlockSpec((1, d), lambda i, j: (0, 0)),
            pl.BlockSpec((None, d, tn), lambda i, j: (layer, 0, j)),
            pl.BlockSpec((None, d, LANES), lambda i, j: (layer, 0, 0)),
        ],
        out_specs=[pl.BlockSpec((tm, tn), lambda i, j: (i, j)),
                   pl.BlockSpec((tm, LANES), lambda i, j: (i, 0))],
        out_shape=[jax.ShapeDtypeStruct((t, n), BF16), jax.ShapeDtypeStruct((t, LANES), F32)],
        scratch_shapes=[pltpu.VMEM((tm, d), BF16)],
        compiler_params=pltpu.CompilerParams(
            dimension_semantics=("parallel", "arbitrary"),
            vmem_limit_bytes=_vmem_limit(blocks, _nbytes((tm, d), BF16))),
        name="ssd_in_proj",
    )(x, nw.reshape(1, d), w, w_dt)


def _sb_in_proj_kernel(x_ref, nw_ref, w_ref, qn_ref, kn_ref, o_ref, xn_ref, *, tiles_per_section):
    j = pl.program_id(1)

    @pl.when(j == 0)
    def _():
        xn_ref[...] = _rms_normalize(x_ref[...], nw_ref[...], NORM_EPS).astype(xn_ref.dtype)

    section = j // tiles_per_section
    heads_per_tile = o_ref.shape[1] // SB_HEAD_DIM

    def project(epilogue):
        for rows in _row_strips(o_ref.shape[0]):
            acc = jnp.dot(xn_ref[rows, :], w_ref[...], preferred_element_type=F32)
            o_ref[rows, :] = epilogue(acc).astype(o_ref.dtype)

    def head_norm(hw_ref, scale):
        def epilogue(acc):
            heads = [_rms_normalize(acc[:, hh * SB_HEAD_DIM:(hh + 1) * SB_HEAD_DIM], hw_ref[...],
                                    NORM_EPS) * scale for hh in range(heads_per_tile)]
            return jnp.concatenate(heads, axis=1)
        return epilogue

    @pl.when(section == 0)
    def _():
        project(head_norm(qn_ref, LOG2E / math.sqrt(SB_HEAD_DIM)))

    @pl.when(section == 1)
    def _():
        project(head_norm(kn_ref, 1.0))

    @pl.when(section == 2)
    def _():
        project(lambda acc: acc)

    @pl.when(section == 3)
    def _():
        project(lambda acc: acc * _sigmoid(acc))


def _sb_in_proj(x, nw, w, layer, qn_w, kn_w, tm_pref=1024, tn_pref=1024):
    t, d = x.shape
    n = w.shape[2]
    width = n // 4
    tm, tn = _pick_tile(t, tm_pref), _pick_tile(width, tn_pref)
    blocks = [_nbytes((tm, d), F32), _nbytes((d, tn), BF16), _nbytes((tm, tn), BF16)]
    return pl.pallas_call(
        functools.partial(_sb_in_proj_kernel, tiles_per_section=width // tn),
        grid=(t // tm, n // tn),
        in_specs=[
            pl.BlockSpec((tm, d), lambda i, j: (_row_block_ahead(i, j >= 1, t // tm), 0)),
            pl.BlockSpec((1, d), lambda i, j: (0, 0)),
            pl.BlockSpec((None, d, tn), lambda i, j: (layer, 0, j)),
            pl.BlockSpec((1, SB_HEAD_DIM), lambda i, j: (0, 0)),
            pl.BlockSpec((1, SB_HEAD_DIM), lambda i, j: (0, 0)),
        ],
        out_specs=pl.BlockSpec((tm, tn), lambda i, j: (i, j)),
        out_shape=jax.ShapeDtypeStruct((t, n), BF16),
        scratch_shapes=[pltpu.VMEM((tm, d), BF16)],
        compiler_params=pltpu.CompilerParams(
            dimension_semantics=("parallel", "arbitrary"),
            vmem_limit_bytes=_vmem_limit(blocks, _nbytes((tm, d), BF16))),
        name="sb_in_proj",
    )(x, nw.reshape(1, d), w, qn_w.reshape(1, SB_HEAD_DIM), kn_w.reshape(1, SB_HEAD_DIM))


def _out_ple_kernel(a_ref, wo_ref, h_ref, nw_ref, gw_ref, p_ref, pw_ref, o_ref, h1_ref, xn_ref, *,
                    n_tiles):
    j = pl.program_id(1)
    tn = o_ref.shape[1]

    strips = _row_strips(o_ref.shape[0])

    def tile(w_ref, cols):
        return w_ref[...] if w_ref.shape[1] == tn else w_ref[:, cols]

    @pl.when(j < n_tiles)
    def _():
        cols = pl.ds(pl.multiple_of(j * tn, tn), tn)
        wo = tile(wo_ref, cols)
        for rows in strips:
            h1_ref[rows, cols] = h_ref[rows, :] + jnp.dot(a_ref[rows, :], wo,
                                                          preferred_element_type=F32)

    @pl.when(j == n_tiles)
    def _():
        xn_ref[...] = _rms_normalize(h1_ref[...], nw_ref[...], NORM_EPS).astype(xn_ref.dtype)

    @pl.when(j >= n_tiles)
    def _():
        cols = pl.ds(pl.multiple_of((j - n_tiles) * tn, tn), tn)
        gw = tile(gw_ref, cols)
        for rows in strips:
            gate = _sigmoid(jnp.dot(xn_ref[rows, :], gw, preferred_element_type=F32))
            emb = jnp.dot(p_ref[rows, :].astype(BF16), pw_ref[...], preferred_element_type=F32)
            o_ref[rows, :] = h1_ref[rows, cols] + emb * gate


def _out_ple(a, wo, mixer_layer, h, nw, gw, p_all, pw, layer, tm_pref=1024, tn_pref=512):
    t, k = a.shape
    d = h.shape[1]
    pd = p_all.shape[1]
    tm, tn = _pick_tile(t, tm_pref), _pick_tile(d, tn_pref)
    nt = d // tn
    p_row0 = layer * (t // tm)
    first = lambda j: jnp.minimum(j, nt - 1)
    second = lambda j: jnp.maximum(j - nt, 0)
    per_step = [_nbytes((tm, k), BF16), _nbytes((tm, tn), F32), _nbytes((tm, pd), F32),
                _nbytes((pd, tn), BF16), _nbytes((tm, tn), F32)]
    scratch = _nbytes((tm, d), F32) + _nbytes((tm, d), BF16)
    whole = _nbytes((k, d), BF16) + _nbytes((d, d), BF16)
    resident = (2 * sum(per_step) + whole + scratch + COMPILER_SCRATCH_BYTES
                <= V7X_VMEM_BYTES - VMEM_RESERVE_BYTES)
    if resident:
        wo_spec = pl.BlockSpec((None, k, d), lambda i, j: (mixer_layer, 0, 0),
                               pipeline_mode=pl.Buffered(1))
        gw_spec = pl.BlockSpec((None, d, d), lambda i, j: (layer, 0, 0),
                               pipeline_mode=pl.Buffered(1))
        limit = _vmem_limit(per_step, scratch + whole)
    else:
        wo_spec = pl.BlockSpec((None, k, tn), lambda i, j: (mixer_layer, 0, first(j)))
        gw_spec = pl.BlockSpec((None, d, tn), lambda i, j: (layer, 0, second(j)))
        limit = _vmem_limit(per_step + [_nbytes((k, tn), BF16), _nbytes((d, tn), BF16)], scratch)
    return pl.pallas_call(
        functools.partial(_out_ple_kernel, n_tiles=nt),
        grid=(t // tm, 2 * nt),
        in_specs=[
            pl.BlockSpec((tm, k), lambda i, j: (_row_block_ahead(i, j >= nt, t // tm), 0)),
            wo_spec,
            pl.BlockSpec((tm, tn), lambda i, j: (i, first(j))),
            pl.BlockSpec((1, d), lambda i, j: (0, 0)),
            gw_spec,
            pl.BlockSpec((tm, pd), lambda i, j: (p_row0 + i, 0)),
            pl.BlockSpec((None, pd, tn), lambda i, j: (layer, 0, second(j))),
        ],
        out_specs=pl.BlockSpec((tm, tn), lambda i, j: (i, second(j))),
        out_shape=jax.ShapeDtypeStruct((t, d), F32),
        scratch_shapes=[pltpu.VMEM((tm, d), F32), pltpu.VMEM((tm, d), BF16)],
        compiler_params=pltpu.CompilerParams(
            dimension_semantics=("parallel", "arbitrary"),
            vmem_limit_bytes=limit),
        name="out_ple",
    )(a, wo, h, nw.reshape(1, d), gw, p_all, pw)


def _sb_attn_kernel(q_ref, k_ref, v_ref, g_ref, o_ref, *, blk, heads_per_step, strips):
    i = pl.program_id(2)
    hd = SB_HEAD_DIM
    row = lax.broadcasted_iota(jnp.int32, (blk, blk), 0)
    col = lax.broadcasted_iota(jnp.int32, (blk, blk), 1)
    strict = col < row
    later = (row > col).astype(BF16)
    strict_b = col.astype(F32).astype(BF16) < row.astype(F32).astype(BF16)

    rows = blk // strips
    units = [(slice(hh * hd, (hh + 1) * hd), slice(st * rows, (st + 1) * rows))
             for hh in range(heads_per_step) for st in range(strips)]

    def block(j, state, diagonal):
        start = pl.multiple_of(j * blk, blk)
        zs, logits, carries, out = {}, {}, {}, {}

        def scores(us):
            for u in us:
                hs, rs = units[u]
                zs[u] = lax.dot_general(q_ref[rs, hs], k_ref[pl.ds(start, blk), hs],
                                        (((1,), (1,)), ((), ())), preferred_element_type=F32)

        def log_terms(us):
            for u in us:
                z = zs[u].astype(BF16)
                log_beta = jnp.minimum(z, 0.0) - jnp.log(1.0 + jnp.exp2(-jnp.abs(z))) * LOG2E
                l1m = log_beta - z
                if diagonal:
                    l1m = jnp.where(strict_b[units[u][1], :], l1m, 0.0)
                rest = jnp.dot(l1m, later, preferred_element_type=F32)
                logits[u] = log_beta.astype(F32) + rest + state[u][0]
                carries[u] = state[u][0] + rest[:, 0:1] + l1m[:, 0:1].astype(F32)

        def weighted_values(us):
            for u in us:
                att = jnp.exp2(logits[u])
                if diagonal:
                    att = jnp.where(strict[units[u][1], :], att, 0.0)
                acc = state[u][1] + jnp.dot(att.astype(BF16), v_ref[pl.ds(start, blk), units[u][0]],
                                            preferred_element_type=F32)
                out[u] = (carries[u], acc)

        per_wave = len(units) // ATTN_WAVES
        waves = [list(range(w * per_wave, (w + 1) * per_wave)) for w in range(ATTN_WAVES)]
        scores(waves[0])
        for w in range(ATTN_WAVES):
            log_terms(waves[w])
            if w + 1 < ATTN_WAVES:
                scores(waves[w + 1])
            weighted_values(waves[w])
        return tuple(out[u] for u in range(len(units)))

    def any_weight_left(state):
        top = state[0][0]
        for u in range(1, len(units)):
            top = jnp.maximum(top, state[u][0])
        return (jnp.max(top, axis=0, keepdims=True)[0, 0] > ATTN_DEAD_LOG2).astype(jnp.int32)

    init = tuple((jnp.zeros((rows, 1), F32), jnp.zeros((rows, hd), F32)) for _ in units)
    state = block(i, init, True)

    def more(c):
        return jnp.logical_and(c[0] < i, c[1] > 0)

    def step(c):
        s = block(i - 1 - c[0], c[2], False)
        return c[0] + 1, any_weight_left(s), s

    state = lax.while_loop(more, step, (jnp.int32(0), any_weight_left(state), state))[2]
    for u, (hs, rs) in enumerate(units):
        o_ref[rs, hs] = (state[u][1] * g_ref[rs, hs].astype(F32)).astype(o_ref.dtype)


def _sb_attention(qkvg, batch, seq, blk_pref=256, heads_per_step=8, strips=1):
    t, n = qkvg.shape
    heads = n // (4 * SB_HEAD_DIM)
    blk = _pick_tile(seq, blk_pref)
    nq = seq // blk
    assert heads % heads_per_step == 0
    hsteps = heads // heads_per_step
    w = heads_per_step * SB_HEAD_DIM
    blocks = [_nbytes((blk, w), BF16), 2 * _nbytes((seq, w), BF16), 2 * _nbytes((blk, w), BF16)]
    return pl.pallas_call(
        functools.partial(_sb_attn_kernel, blk=blk, heads_per_step=heads_per_step, strips=strips),
        grid=(batch, hsteps, nq),
        in_specs=[
            pl.BlockSpec((blk, w), lambda b, h, i: (b * nq + i, h)),
            pl.BlockSpec((seq, w), lambda b, h, i: (b, hsteps + h)),
            pl.BlockSpec((seq, w), lambda b, h, i: (b, 2 * hsteps + h)),
            pl.BlockSpec((blk, w), lambda b, h, i: (b * nq + i, 3 * hsteps + h)),
        ],
        out_specs=pl.BlockSpec((blk, w), lambda b, h, i: (b * nq + i, h)),
        out_shape=jax.ShapeDtypeStruct((t, heads * SB_HEAD_DIM), BF16),
        compiler_params=pltpu.CompilerParams(
            dimension_semantics=("parallel", "parallel", "arbitrary"),
            vmem_limit_bytes=_vmem_limit(blocks)),
        name="sb_attention",
    )(qkvg, qkvg, qkvg, qkvg)


def _split3(v):
    hi = v.astype(BF16)
    r = v - hi.astype(F32)
    mid = r.astype(BF16)
    lo = (r - mid.astype(F32)).astype(BF16)
    return hi, mid, lo


def _ssd_kernel(z_ref, x_ref, b_ref, c_ref, dt_ref, spread_ref, shift_ref, convw_ref, convb_ref,
                dtb_ref, alog_ref, dexp_ref, gnw_ref, o_ref,
                tail_ref, state_ref, acum_sp_ref, ea_sp_ref, rowb_ref, wt_ref, *, d_inner):
    chunk = SSD_CHUNK
    heads_per_group = d_inner // SSD_N_GROUPS // SSD_HEAD_DIM
    gw = d_inner // SSD_N_GROUPS
    nbc = SSD_N_GROUPS * SSD_D_STATE
    first = pl.program_id(1) == 0

    @pl.when(first)
    def _():
        tail_ref[...] = jnp.zeros(tail_ref.shape, tail_ref.dtype)
        state_ref[...] = jnp.zeros(state_ref.shape, F32)

    dt = _softplus(dt_ref[...] + dtb_ref[...])
    adt = dt * (-LOG2E * jnp.exp(alog_ref[...]))
    li = lax.broadcasted_iota(jnp.int32, (chunk, chunk), 0)
    si = lax.broadcasted_iota(jnp.int32, (chunk, chunk), 1)
    causal = si <= li
    tri = causal.astype(BF16)
    acum = sum(jnp.dot(tri, piece, preferred_element_type=F32) for piece in _split3(adt))
    acum_t = acum.T
    dt_t = dt.T
    rowb_ref[...] = acum_t - jnp.log(dt_t) * LOG2E
    wt_ref[...] = dt_t * jnp.exp2(acum_t[:, chunk - 1:chunk] - acum_t)
    acum_sp_ref[...] = sum(jnp.dot(piece, spread_ref[...], preferred_element_type=F32)
                           for piece in _split3(acum))
    ea_sp_ref[...] = sum(jnp.dot(piece, spread_ref[...], preferred_element_type=F32)
                         for piece in _split3(jnp.exp2(acum)))

    lane_lo = lax.broadcasted_iota(jnp.int32, (chunk, LANES), 1) < SSD_HEAD_DIM

    def conv_silu(g):
        xcol = g * gw
        ncol = g * SSD_D_STATE
        srcs = ((x_ref, xcol, xcol, gw), (b_ref, ncol, d_inner + ncol, SSD_D_STATE),
                (c_ref, ncol, d_inner + nbc + ncol, SSD_D_STATE))
        raw = jnp.concatenate(
            [jnp.concatenate([tail_ref[:, pl.ds(tcol, width)], ref[:, pl.ds(col, width)]], axis=0)
             for ref, col, tcol, width in srcs], axis=1)
        w = jnp.concatenate([convw_ref[:, pl.ds(tcol, width)] for _, _, tcol, width in srcs], axis=1)
        acc = jnp.concatenate([convb_ref[:, pl.ds(tcol, width)] for _, _, tcol, width in srcs], axis=1)
        shifted = jnp.dot(shift_ref[...], raw, preferred_element_type=F32)
        for tap in range(SSD_D_CONV):
            k = SSD_D_CONV - 1 - tap
            acc = acc + w[tap:tap + 1, :] * shifted[k * chunk:(k + 1) * chunk, :]
        return acc * _sigmoid(acc)

    def scan_group(g, act):
        xcol = g * gw
        ncol = g * SSD_D_STATE
        xs = act[:, 0:gw]
        bg = act[:, gw:gw + SSD_D_STATE]
        cg = act[:, gw + SSD_D_STATE:gw + 2 * SSD_D_STATE]
        scores = lax.dot_general(cg.astype(BF16), bg.astype(BF16), (((1,), (1,)), ((), ())),
                                 preferred_element_type=F32)
        bg_t = bg.T
        hrow = g * SUBLANES
        acum_g = acum_sp_ref[:, pl.ds(ncol, LANES)]
        ea_g = ea_sp_ref[:, pl.ds(ncol, LANES)]
        rowb_g = rowb_ref[pl.ds(hrow, heads_per_group), :]
        wt_g = wt_ref[pl.ds(hrow, heads_per_group), :]
        st = state_ref[g]
        y_off = jnp.dot(cg.astype(BF16), st.astype(BF16), preferred_element_type=F32)

        pieces = []
        for pr in range(heads_per_group // 2):
            lhs_diag, lhs_state, ea_cols = [], [], []
            for r in (2 * pr, 2 * pr + 1):
                colb = jnp.broadcast_to(acum_g[:, r:r + 1], (chunk, chunk))
                decay_dt = jnp.where(causal, jnp.exp2(colb - rowb_g[r:r + 1, :]), 0.0)
                lhs_diag.append((decay_dt * scores).astype(BF16))
                lhs_state.append((bg_t * wt_g[r:r + 1, :]).astype(BF16))
                ea_cols.append(jnp.broadcast_to(ea_g[:, r:r + 1], (chunk, LANES)))
            psl = slice(pr * LANES, (pr + 1) * LANES)
            xp = xs[:, psl]
            x_bd = jnp.concatenate([jnp.where(lane_lo, xp, 0.0).astype(BF16),
                                    jnp.where(lane_lo, 0.0, xp).astype(BF16)], axis=0)
            ea_pair = jnp.where(lane_lo, ea_cols[0], ea_cols[1])
            y_pair = (jnp.dot(jnp.concatenate(lhs_diag, axis=1), x_bd, preferred_element_type=F32)
                      + ea_pair * y_off[:, psl])
            contrib = jnp.dot(jnp.concatenate(lhs_state, axis=1), x_bd, preferred_element_type=F32)
            state_ref[g, :, psl] = st[:, psl] * ea_pair[chunk - 1:chunk, :] + contrib
            pieces.append(y_pair)

        xsl = pl.ds(xcol, gw)
        y = jnp.concatenate(pieces, axis=1) + dexp_ref[:, xsl] * xs
        zt = z_ref[:, xsl].astype(F32)
        y = y * (zt * _sigmoid(zt))
        o_ref[:, xsl] = _rms_normalize(y, gnw_ref[:, xsl], GATED_NORM_EPS).astype(o_ref.dtype)

    act = conv_silu(0)
    for g in range(SSD_N_GROUPS):
        act_next = conv_silu(g + 1) if g + 1 < SSD_N_GROUPS else None
        scan_group(g, act)
        act = act_next

    keep = slice(chunk - BF16_SUBLANES, chunk)
    tail_ref[keep, 0:d_inner] = x_ref[keep, :]
    tail_ref[keep, d_inner:d_inner + nbc] = b_ref[keep, :]
    tail_ref[keep, d_inner + nbc:d_inner + 2 * nbc] = c_ref[keep, :]


def _ssd_mixer(proj, dt_raw, conv_w, conv_b, dt_bias_pad, a_log_pad, d_exp, gnorm_w, batch, seq,
               d_inner):
    t = proj.shape[0]
    chunk = SSD_CHUNK
    nc = seq // chunk
    nbc = SSD_N_GROUPS * SSD_D_STATE
    conv_dim = d_inner + 2 * nbc
    gw = d_inner // SSD_N_GROUPS
    heads_per_group = gw // SSD_HEAD_DIM
    assert heads_per_group == SUBLANES and d_inner % nbc == 0
    assert proj.dtype == BF16
    hh = jnp.arange(LANES)[:, None]
    cc = jnp.arange(SSD_N_GROUPS * LANES)[None, :]
    spread = ((cc % LANES < heads_per_group)
              & (hh == (cc // LANES) * heads_per_group + cc % LANES)).astype(BF16)
    rr = jnp.arange(SSD_D_CONV * chunk)[:, None]
    shift = (jnp.arange(2 * chunk)[None, :] == chunk + rr % chunk - rr // chunk).astype(BF16)
    row = lambda b, c: b * nc + c
    blocks = [2 * _nbytes((chunk, d_inner), proj.dtype), 2 * _nbytes((chunk, nbc), proj.dtype),
              _nbytes((chunk, LANES), F32), _nbytes(spread.shape, BF16), _nbytes(shift.shape, BF16),
              _nbytes((8, conv_dim), F32) * 2, _nbytes((chunk, d_inner), BF16)]
    scratch = (_nbytes((chunk, conv_dim), BF16) + _nbytes((SSD_N_GROUPS, SSD_D_STATE, gw), F32)
               + 2 * _nbytes((chunk, SSD_N_GROUPS * LANES), F32) + 2 * _nbytes((LANES, chunk), F32))
    full = lambda shape: pl.BlockSpec(shape, lambda b, c: (0, 0))
    return pl.pallas_call(
        functools.partial(_ssd_kernel, d_inner=d_inner),
        grid=(batch, nc),
        in_specs=[
            pl.BlockSpec((chunk, d_inner), lambda b, c: (row(b, c), 0)),
            pl.BlockSpec((chunk, d_inner), lambda b, c: (row(b, c), 1)),
            pl.BlockSpec((chunk, nbc), lambda b, c: (row(b, c), 2 * d_inner // nbc)),
            pl.BlockSpec((chunk, nbc), lambda b, c: (row(b, c), 2 * d_inner // nbc + 1)),
            pl.BlockSpec((chunk, LANES), lambda b, c: (row(b, c), 0)),
            full(spread.shape),
            full(shift.shape),
            full((SSD_D_CONV, conv_dim)),
            full((1, conv_dim)),
            full((1, LANES)),
            full((1, LANES)),
            full((1, d_inner)),
            full((1, d_inner)),
        ],
        out_specs=pl.BlockSpec((chunk, d_inner), lambda b, c: (row(b, c), 0)),
        out_shape=jax.ShapeDtypeStruct((t, d_inner), BF16),
        scratch_shapes=[
            pltpu.VMEM((chunk, conv_dim), proj.dtype),
            pltpu.VMEM((SSD_N_GROUPS, SSD_D_STATE, gw), F32),
            pltpu.VMEM((chunk, SSD_N_GROUPS * LANES), F32),
            pltpu.VMEM((chunk, SSD_N_GROUPS * LANES), F32),
            pltpu.VMEM((LANES, chunk), F32),
            pltpu.VMEM((LANES, chunk), F32),
        ],
        compiler_params=pltpu.CompilerParams(
            dimension_semantics=("parallel", "arbitrary"),
            vmem_limit_bytes=_vmem_limit(blocks, scratch)),
        name="ssd_mixer",
    )(proj, proj, proj, proj, dt_raw, spread, shift, conv_w, conv_b.reshape(1, conv_dim),
      dt_bias_pad, a_log_pad, d_exp, gnorm_w.reshape(1, d_inner))


def _pad_lanes(v):
    return jnp.pad(v.astype(F32), (0, LANES - v.shape[0])).reshape(1, LANES)


def _ssd_mix(h, nw, in_w, dt_w, layer, conv_w, conv_b, dt_bias, a_log, d_skip, gnorm_w, d_inner,
             batch, seq):
    main = 2 * d_inner + 2 * SSD_N_GROUPS * SSD_D_STATE
    proj, dt_raw = _ssd_in_proj(h, nw, in_w, dt_w, layer, main)
    d_exp = jnp.repeat(d_skip.astype(F32), SSD_HEAD_DIM).reshape(1, d_inner)
    return _ssd_mixer(proj, dt_raw, conv_w, conv_b, _pad_lanes(dt_bias), _pad_lanes(a_log), d_exp,
                      gnorm_w, batch, seq, d_inner)


def kernel(x, p, norm_w, ssd_in_w, ssd_conv_w, ssd_conv_b, ssd_dt_bias, ssd_a_log, ssd_d, ssd_gnorm_w, ssd_out_w, sb_in_w, sb_qn_w, sb_kn_w, sb_out_w, ple_norm_w, ple_gate_w, ple_proj_w):
    batch, seq, d_model = x.shape
    depth = p.shape[0]
    d_inner = ssd_out_w.shape[1]
    main = 2 * d_inner + 2 * SSD_N_GROUPS * SSD_D_STATE
    n_ssd_heads = ssd_in_w.shape[2] - main
    ssd_in_b = ssd_in_w.astype(BF16)
    ssd_dt_b = jnp.pad(ssd_in_w[:, :, main:], ((0, 0), (0, 0), (0, LANES - n_ssd_heads))).astype(BF16)
    ssd_out_b = ssd_out_w.astype(BF16)
    sb_in_b = sb_in_w.astype(BF16)
    sb_out_b = sb_out_w.astype(BF16)
    gate_b = ple_gate_w.astype(BF16)
    proj_b = ple_proj_w.astype(BF16)
    p_all = p.reshape(depth * batch * seq, p.shape[-1])
    h = x.reshape(batch * seq, d_model)
    for i in range(depth):
        j = i // 2
        if i % 2 == 0:
            a = _ssd_mix(h, norm_w[i], ssd_in_b, ssd_dt_b, j, ssd_conv_w[j], ssd_conv_b[j],
                         ssd_dt_bias[j], ssd_a_log[j], ssd_d[j], ssd_gnorm_w[j], d_inner, batch, seq)
            wo = ssd_out_b
        else:
            qkvg = _sb_in_proj(h, norm_w[i], sb_in_b, j, sb_qn_w[j], sb_kn_w[j])
            a = _sb_attention(qkvg, batch, seq)
            wo = sb_out_b
        h = _out_ple(a, wo, j, h, ple_norm_w[i], gate_b, p_all, proj_b, i)
    return h.reshape(batch, seq, d_model)
```

```python
import functools
import math

import jax
import jax.numpy as jnp
from jax import lax
from jax.experimental import pallas as pl
from jax.experimental.pallas import tpu as pltpu

F32 = jnp.float32
BF16 = jnp.bfloat16

NORM_EPS = 1e-6
GATED_NORM_EPS = 1e-5
LOG2E = 1.4426950408889634
SIGN_BIT = 0x80000000

SSD_HEAD_DIM = 64
SSD_N_GROUPS = 8
SSD_D_STATE = 128
SSD_D_CONV = 4
SSD_CHUNK = 128
SB_HEAD_DIM = 128

LANES = 128
SUBLANES = 8
BF16_SUBLANES = 16
V7X_VMEM_BYTES = 64 * 1024 * 1024
COMPILER_SCRATCH_BYTES = 8 * 1024 * 1024
VMEM_RESERVE_BYTES = 8 * 1024 * 1024


def _vmem_limit(block_bytes, scratch_bytes=0):
    need = 2 * sum(block_bytes) + scratch_bytes + COMPILER_SCRATCH_BYTES
    return int(min(need, V7X_VMEM_BYTES - VMEM_RESERVE_BYTES))


def _nbytes(shape, dtype):
    return math.prod(shape) * jnp.dtype(dtype).itemsize


def _pick_tile(n, pref):
    t = min(n, pref)
    assert n % t == 0, (n, t)
    return t


def _sigmoid(x):
    return 1.0 / (1.0 + jnp.exp(-x))


def _softplus(x):
    return jnp.maximum(x, 0.0) + jnp.log(1.0 + jnp.exp(-jnp.abs(x)))


def _rms_normalize(x, w, eps):
    ms = jnp.mean(x * x, axis=-1, keepdims=True)
    return x * lax.rsqrt(ms + eps) * w


def _row_block_ahead(i, done_with_block, n_blocks):
    return jnp.where(done_with_block, jnp.minimum(i + 1, n_blocks - 1), i)


STRIP_ROWS = 256
ATTN_WAVES = 2
ATTN_DEAD_LOG2 = -152.0


def _row_strips(rows):
    step = min(rows, STRIP_ROWS)
    return [slice(r, r + step) for r in range(0, rows, step)]


def _ssd_in_proj_kernel(x_ref, nw_ref, w_ref, wdt_ref, o_ref, dt_ref, xn_ref):
    @pl.when(pl.program_id(1) == 0)
    def _():
        xn_ref[...] = _rms_normalize(x_ref[...], nw_ref[...], NORM_EPS).astype(xn_ref.dtype)
        dt_ref[...] = jnp.dot(xn_ref[...], wdt_ref[...], preferred_element_type=F32)

    o_ref[...] = jnp.dot(xn_ref[...], w_ref[...], preferred_element_type=F32).astype(o_ref.dtype)


def _ssd_in_proj(x, nw, w, w_dt, layer, n, tm_pref=1024, tn_pref=2048):
    t, d = x.shape
    tm, tn = _pick_tile(t, tm_pref), _pick_tile(n, tn_pref)
    blocks = [_nbytes((tm, d), F32), _nbytes((d, tn), BF16), _nbytes((d, LANES), BF16),
              _nbytes((tm, tn), BF16), _nbytes((tm, LANES), F32)]
    return pl.pallas_call(
        _ssd_in_proj_kernel,
        grid=(t // tm, n // tn),
        in_specs=[
            pl.BlockSpec((tm, d), lambda i, j: (_row_block_ahead(i, j >= 1, t // tm), 0)),
            pl.BlockSpec((1, d), lambda i, j: (0, 0)),
            pl.BlockSpec((None, d, tn), lambda i, j: (layer, 0, j)),
            pl.BlockSpec((None, d, LANES), lambda i, j: (layer, 0, 0)),
        ],
        out_specs=[pl.BlockSpec((tm, tn), lambda i, j: (i, j)),
                   pl.BlockSpec((tm, LANES), lambda i, j: (i, 0))],
        out_shape=[jax.ShapeDtypeStruct((t, n), BF16), jax.ShapeDtypeStruct((t, LANES), F32)],
        scratch_shapes=[pltpu.VMEM((tm, d), BF16)],
        compiler_params=pltpu.CompilerParams(
            dimension_semantics=("parallel", "arbitrary"),
            vmem_limit_bytes=_vmem_limit(blocks, _nbytes((tm, d), BF16))),
        name="ssd_in_proj",
    )(x, nw.reshape(1, d), w, w_dt)


def _sb_in_proj_kernel(x_ref, nw_ref, w_ref, qn_ref, kn_ref, o_ref, xn_ref, *, tiles_per_section):
    j = pl.program_id(1)

    @pl.when(j == 0)
    def _():
        xn_ref[...] = _rms_normalize(x_ref[...], nw_ref[...], NORM_EPS).astype(xn_ref.dtype)

    section = j // tiles_per_section
    heads_per_tile = o_ref.shape[1] // SB_HEAD_DIM

    def project(epilogue):
        for rows in _row_strips(o_ref.shape[0]):
            acc = jnp.dot(xn_ref[rows, :], w_ref[...], preferred_element_type=F32)
            o_ref[rows, :] = epilogue(acc).astype(o_ref.dtype)

    def head_norm(hw_ref, scale):
        def epilogue(acc):
            heads = [_rms_normalize(acc[:, hh * SB_HEAD_DIM:(hh + 1) * SB_HEAD_DIM], hw_ref[...],
                                    NORM_EPS) * scale for hh in range(heads_per_tile)]
            return jnp.concatenate(heads, axis=1)
        return epilogue

    @pl.when(section == 0)
    def _():
        project(head_norm(qn_ref, LOG2E / math.sqrt(SB_HEAD_DIM)))

    @pl.when(section == 1)
    def _():
        project(head_norm(kn_ref, 1.0))

    @pl.when(section == 2)
    def _():
        project(lambda acc: acc)

    @pl.when(section == 3)
    def _():
        project(lambda acc: acc * _sigmoid(acc))


def _sb_in_proj(x, nw, w, layer, qn_w, kn_w, tm_pref=1024, tn_pref=2048):
    t, d = x.shape
    n = w.shape[2]
    width = n // 4
    tm, tn = _pick_tile(t, tm_pref), _pick_tile(width, tn_pref)
    blocks = [_nbytes((tm, d), F32), _nbytes((d, tn), BF16), _nbytes((tm, tn), BF16)]
    return pl.pallas_call(
        functools.partial(_sb_in_proj_kernel, tiles_per_section=width // tn),
        grid=(t // tm, n // tn),
        in_specs=[
            pl.BlockSpec((tm, d), lambda i, j: (_row_block_ahead(i, j >= 1, t // tm), 0)),
            pl.BlockSpec((1, d), lambda i, j: (0, 0)),
            pl.BlockSpec((None, d, tn), lambda i, j: (layer, 0, j)),
            pl.BlockSpec((1, SB_HEAD_DIM), lambda i, j: (0, 0)),
            pl.BlockSpec((1, SB_HEAD_DIM), lambda i, j: (0, 0)),
        ],
        out_specs=pl.BlockSpec((tm, tn), lambda i, j: (i, j)),
        out_shape=jax.ShapeDtypeStruct((t, n), BF16),
        scratch_shapes=[pltpu.VMEM((tm, d), BF16)],
        compiler_params=pltpu.CompilerParams(
            dimension_semantics=("parallel", "arbitrary"),
            vmem_limit_bytes=_vmem_limit(blocks, _nbytes((tm, d), BF16))),
        name="sb_in_proj",
    )(x, nw.reshape(1, d), w, qn_w.reshape(1, SB_HEAD_DIM), kn_w.reshape(1, SB_HEAD_DIM))


def _out_ple_kernel(a_ref, wo_ref, h_ref, nw_ref, gw_ref, p_ref, pw_ref, o_ref, h1_ref, xn_ref, *,
                    n_tiles):
    j = pl.program_id(1)
    tn = o_ref.shape[1]

    strips = _row_strips(o_ref.shape[0])

    def tile(w_ref, cols):
        return w_ref[...] if w_ref.shape[1] == tn else w_ref[:, cols]

    @pl.when(j < n_tiles)
    def _():
        cols = pl.ds(pl.multiple_of(j * tn, tn), tn)
        wo = tile(wo_ref, cols)
        for rows in strips:
            h1_ref[rows, cols] = h_ref[rows, :] + jnp.dot(a_ref[rows, :], wo,
                                                          preferred_element_type=F32)

    @pl.when(j == n_tiles)
    def _():
        xn_ref[...] = _rms_normalize(h1_ref[...], nw_ref[...], NORM_EPS).astype(xn_ref.dtype)

    @pl.when(j >= n_tiles)
    def _():
        cols = pl.ds(pl.multiple_of((j - n_tiles) * tn, tn), tn)
        gw = tile(gw_ref, cols)
        for rows in strips:
            gate = _sigmoid(jnp.dot(xn_ref[rows, :], gw, preferred_element_type=F32))
            emb = jnp.dot(p_ref[rows, :].astype(BF16), pw_ref[...], preferred_element_type=F32)
            o_ref[rows, :] = h1_ref[rows, cols] + emb * gate


def _out_ple(a, wo, mixer_layer, h, nw, gw, p_all, pw, layer, tm_pref=1024, tn_pref=512):
    t, k = a.shape
    d = h.shape[1]
    pd = p_all.shape[1]
    tm, tn = _pick_tile(t, tm_pref), _pick_tile(d, tn_pref)
    nt = d // tn
    p_row0 = layer * (t // tm)
    first = lambda j: jnp.minimum(j, nt - 1)
    second = lambda j: jnp.maximum(j - nt, 0)
    per_step = [_nbytes((tm, k), BF16), _nbytes((tm, tn), F32), _nbytes((tm, pd), F32),
                _nbytes((pd, tn), BF16), _nbytes((tm, tn), F32)]
    scratch = _nbytes((tm, d), F32) + _nbytes((tm, d), BF16)
    whole = _nbytes((k, d), BF16) + _nbytes((d, d), BF16)
    resident = (2 * sum(per_step) + whole + scratch + COMPILER_SCRATCH_BYTES
                <= V7X_VMEM_BYTES - VMEM_RESERVE_BYTES)
    if resident:
        wo_spec = pl.BlockSpec((None, k, d), lambda i, j: (mixer_layer, 0, 0),
                               pipeline_mode=pl.Buffered(1))
        gw_spec = pl.BlockSpec((None, d, d), lambda i, j: (layer, 0, 0),
                               pipeline_mode=pl.Buffered(1))
        limit = _vmem_limit(per_step, scratch + whole)
    else:
        wo_spec = pl.BlockSpec((None, k, tn), lambda i, j: (mixer_layer, 0, first(j)))
        gw_spec = pl.BlockSpec((None, d, tn), lambda i, j: (layer, 0, second(j)))
        limit = _vmem_limit(per_step + [_nbytes((k, tn), BF16), _nbytes((d, tn), BF16)], scratch)
    return pl.pallas_call(
        functools.partial(_out_ple_kernel, n_tiles=nt),
        grid=(t // tm, 2 * nt),
        in_specs=[
            pl.BlockSpec((tm, k), lambda i, j: (_row_block_ahead(i, j >= nt, t // tm), 0)),
            wo_spec,
            pl.BlockSpec((tm, tn), lambda i, j: (i, first(j))),
            pl.BlockSpec((1, d), lambda i, j: (0, 0)),
            gw_spec,
            pl.BlockSpec((tm, pd), lambda i, j: (p_row0 + i, 0)),
            pl.BlockSpec((None, pd, tn), lambda i, j: (layer, 0, second(j))),
        ],
        out_specs=pl.BlockSpec((tm, tn), lambda i, j: (i, second(j))),
        out_shape=jax.ShapeDtypeStruct((t, d), F32),
        scratch_shapes=[pltpu.VMEM((tm, d), F32), pltpu.VMEM((tm, d), BF16)],
        compiler_params=pltpu.CompilerParams(
            dimension_semantics=("parallel", "arbitrary"),
            vmem_limit_bytes=limit),
        name="out_ple",
    )(a, wo, h, nw.reshape(1, d), gw, p_all, pw)


def _sb_attn_kernel(q_ref, k_ref, v_ref, g_ref, o_ref, *, blk, heads_per_step):
    i = pl.program_id(2)
    hd = SB_HEAD_DIM
    row = lax.broadcasted_iota(jnp.int32, (blk, blk), 0)
    col = lax.broadcasted_iota(jnp.int32, (blk, blk), 1)
    strict = col < row
    later = (row > col).astype(BF16)
    strict_b = col.astype(F32).astype(BF16) < row.astype(F32).astype(BF16)

    units = [slice(hh * hd, (hh + 1) * hd) for hh in range(heads_per_step)]

    def block(j, state, diagonal):
        start = pl.multiple_of(j * blk, blk)
        zs, logits, carries, out = {}, {}, {}, {}

        def scores(us):
            for u in us:
                hs = units[u]
                zs[u] = lax.dot_general(q_ref[:, hs], k_ref[pl.ds(start, blk), hs],
                                        (((1,), (1,)), ((), ())), preferred_element_type=F32)

        def log_terms(us):
            for u in us:
                z = zs[u].astype(BF16)
                log_beta = jnp.minimum(z, 0.0) - jnp.log(1.0 + jnp.exp2(-jnp.abs(z))) * LOG2E
                l1m = log_beta - z
                if diagonal:
                    l1m = jnp.where(strict_b, l1m, 0.0)
                rest = jnp.dot(l1m, later, preferred_element_type=F32)
                logits[u] = log_beta.astype(F32) + rest + state[u][0]
                carries[u] = state[u][0] + rest[:, 0:1] + l1m[:, 0:1].astype(F32)

        def weighted_values(us):
            for u in us:
                att = jnp.exp2(logits[u])
                if diagonal:
                    att = jnp.where(strict, att, 0.0)
                acc = state[u][1] + jnp.dot(att.astype(BF16), v_ref[pl.ds(start, blk), units[u]],
                                            preferred_element_type=F32)
                out[u] = (carries[u], acc)

        per_wave = len(units) // ATTN_WAVES
        waves = [list(range(w * per_wave, (w + 1) * per_wave)) for w in range(ATTN_WAVES)]
        scores(waves[0])
        for w in range(ATTN_WAVES):
            log_terms(waves[w])
            if w + 1 < ATTN_WAVES:
                scores(waves[w + 1])
            weighted_values(waves[w])
        return tuple(out[u] for u in range(len(units)))

    def any_weight_left(state):
        top = state[0][0]
        for u in range(1, len(units)):
            top = jnp.maximum(top, state[u][0])
        return (jnp.max(top, axis=0, keepdims=True)[0, 0] > ATTN_DEAD_LOG2).astype(jnp.int32)

    init = tuple((jnp.zeros((blk, 1), F32), jnp.zeros((blk, hd), F32)) for _ in units)
    state = block(i, init, True)

    def more(c):
        return jnp.logical_and(c[0] < i, c[1] > 0)

    def step(c):
        s = block(i - 1 - c[0], c[2], False)
        return c[0] + 1, any_weight_left(s), s

    state = lax.while_loop(more, step, (jnp.int32(0), any_weight_left(state), state))[2]
    for u, hs in enumerate(units):
        o_ref[:, hs] = (state[u][1] * g_ref[:, hs].astype(F32)).astype(o_ref.dtype)


def _sb_attention(qkvg, batch, seq, blk_pref=256, heads_per_step=8):
    t, n = qkvg.shape
    heads = n // (4 * SB_HEAD_DIM)
    blk = _pick_tile(seq, blk_pref)
    nq = seq // blk
    assert heads % heads_per_step == 0
    hsteps = heads // heads_per_step
    w = heads_per_step * SB_HEAD_DIM
    blocks = [_nbytes((blk, w), BF16), 2 * _nbytes((seq, w), BF16), 2 * _nbytes((blk, w), BF16)]
    return pl.pallas_call(
        functools.partial(_sb_attn_kernel, blk=blk, heads_per_step=heads_per_step),
        grid=(batch, hsteps, nq),
        in_specs=[
            pl.BlockSpec((blk, w), lambda b, h, i: (b * nq + i, h)),
            pl.BlockSpec((seq, w), lambda b, h, i: (b, hsteps + h)),
            pl.BlockSpec((seq, w), lambda b, h, i: (b, 2 * hsteps + h)),
            pl.BlockSpec((blk, w), lambda b, h, i: (b * nq + i, 3 * hsteps + h)),
        ],
        out_specs=pl.BlockSpec((blk, w), lambda b, h, i: (b * nq + i, h)),
        out_shape=jax.ShapeDtypeStruct((t, heads * SB_HEAD_DIM), BF16),
        compiler_params=pltpu.CompilerParams(
            dimension_semantics=("parallel", "parallel", "arbitrary"),
            vmem_limit_bytes=_vmem_limit(blocks)),
        name="sb_attention",
    )(qkvg, qkvg, qkvg, qkvg)


def _split3(v):
    hi = v.astype(BF16)
    r = v - hi.astype(F32)
    mid = r.astype(BF16)
    lo = (r - mid.astype(F32)).astype(BF16)
    return hi, mid, lo


def _ssd_kernel(z_ref, x_ref, b_ref, c_ref, dt_ref, spread_ref, shift_ref, convw_ref, convb_ref,
                dtb_ref, alog_ref, dexp_ref, gnw_ref, o_ref,
                tail_ref, state_ref, acum_sp_ref, ea_sp_ref, rowb_ref, wt_ref, *, d_inner):
    chunk = SSD_CHUNK
    heads_per_group = d_inner // SSD_N_GROUPS // SSD_HEAD_DIM
    gw = d_inner // SSD_N_GROUPS
    nbc = SSD_N_GROUPS * SSD_D_STATE
    first = pl.program_id(1) == 0

    @pl.when(first)
    def _():
        tail_ref[...] = jnp.zeros(tail_ref.shape, tail_ref.dtype)
        state_ref[...] = jnp.zeros(state_ref.shape, F32)

    dt = _softplus(dt_ref[...] + dtb_ref[...])
    adt = dt * (-LOG2E * jnp.exp(alog_ref[...]))
    li = lax.broadcasted_iota(jnp.int32, (chunk, chunk), 0)
    si = lax.broadcasted_iota(jnp.int32, (chunk, chunk), 1)
    causal = si <= li
    tri = causal.astype(BF16)
    acum = sum(jnp.dot(tri, piece, preferred_element_type=F32) for piece in _split3(adt))
    acum_t = acum.T
    dt_t = dt.T
    rowb_ref[...] = acum_t - jnp.log(dt_t) * LOG2E
    wt_ref[...] = dt_t * jnp.exp2(acum_t[:, chunk - 1:chunk] - acum_t)
    acum_sp_ref[...] = sum(jnp.dot(piece, spread_ref[...], preferred_element_type=F32)
                           for piece in _split3(acum))
    ea_sp_ref[...] = sum(jnp.dot(piece, spread_ref[...], preferred_element_type=F32)
                         for piece in _split3(jnp.exp2(acum)))

    lane_lo = lax.broadcasted_iota(jnp.int32, (chunk, LANES), 1) < SSD_HEAD_DIM

    def conv_silu(g):
        xcol = g * gw
        ncol = g * SSD_D_STATE
        srcs = ((x_ref, xcol, xcol, gw), (b_ref, ncol, d_inner + ncol, SSD_D_STATE),
                (c_ref, ncol, d_inner + nbc + ncol, SSD_D_STATE))
        raw = jnp.concatenate(
            [jnp.concatenate([tail_ref[:, pl.ds(tcol, width)], ref[:, pl.ds(col, width)]], axis=0)
             for ref, col, tcol, width in srcs], axis=1)
        w = jnp.concatenate([convw_ref[:, pl.ds(tcol, width)] for _, _, tcol, width in srcs], axis=1)
        acc = jnp.concatenate([convb_ref[:, pl.ds(tcol, width)] for _, _, tcol, width in srcs], axis=1)
        shifted = jnp.dot(shift_ref[...], raw, preferred_element_type=F32)
        for tap in range(SSD_D_CONV):
            k = SSD_D_CONV - 1 - tap
            acc = acc + w[tap:tap + 1, :] * shifted[k * chunk:(k + 1) * chunk, :]
        return acc * _sigmoid(acc)

    def scan_group(g, act):
        xcol = g * gw
        ncol = g * SSD_D_STATE
        xs = act[:, 0:gw]
        bg = act[:, gw:gw + SSD_D_STATE]
        cg = act[:, gw + SSD_D_STATE:gw + 2 * SSD_D_STATE]
        scores = lax.dot_general(cg.astype(BF16), bg.astype(BF16), (((1,), (1,)), ((), ())),
                                 preferred_element_type=F32)
        bg_t = bg.T
        hrow = g * SUBLANES
        acum_g = acum_sp_ref[:, pl.ds(ncol, LANES)]
        ea_g = ea_sp_ref[:, pl.ds(ncol, LANES)]
        rowb_g = rowb_ref[pl.ds(hrow, heads_per_group), :]
        wt_g = wt_ref[pl.ds(hrow, heads_per_group), :]
        st = state_ref[g]
        y_off = jnp.dot(cg.astype(BF16), st.astype(BF16), preferred_element_type=F32)

        pieces = []
        for pr in range(heads_per_group // 2):
            lhs_diag, lhs_state, ea_cols = [], [], []
            for r in (2 * pr, 2 * pr + 1):
                colb = jnp.broadcast_to(acum_g[:, r:r + 1], (chunk, chunk))
                decay_dt = jnp.where(causal, jnp.exp2(colb - rowb_g[r:r + 1, :]), 0.0)
                lhs_diag.append((decay_dt * scores).astype(BF16))
                lhs_state.append((bg_t * wt_g[r:r + 1, :]).astype(BF16))
                ea_cols.append(jnp.broadcast_to(ea_g[:, r:r + 1], (chunk, LANES)))
            psl = slice(pr * LANES, (pr + 1) * LANES)
            xp = xs[:, psl]
            x_bd = jnp.concatenate([jnp.where(lane_lo, xp, 0.0).astype(BF16),
                                    jnp.where(lane_lo, 0.0, xp).astype(BF16)], axis=0)
            ea_pair = jnp.where(lane_lo, ea_cols[0], ea_cols[1])
            y_pair = (jnp.dot(jnp.concatenate(lhs_diag, axis=1), x_bd, preferred_element_type=F32)
                      + ea_pair * y_off[:, psl])
            contrib = jnp.dot(jnp.concatenate(lhs_state, axis=1), x_bd, preferred_element_type=F32)
            state_ref[g, :, psl] = st[:, psl] * ea_pair[chunk - 1:chunk, :] + contrib
            pieces.append(y_pair)

        xsl = pl.ds(xcol, gw)
        y = jnp.concatenate(pieces, axis=1) + dexp_ref[:, xsl] * xs
        zt = z_ref[:, xsl].astype(F32)
        y = y * (zt * _sigmoid(zt))
        o_ref[:, xsl] = _rms_normalize(y, gnw_ref[:, xsl], GATED_NORM_EPS).astype(o_ref.dtype)

    act = conv_silu(0)
    for g in range(SSD_N_GROUPS):
        act_next = conv_silu(g + 1) if g + 1 < SSD_N_GROUPS else None
        scan_group(g, act)
        act = act_next

    keep = slice(chunk - BF16_SUBLANES, chunk)
    tail_ref[keep, 0:d_inner] = x_ref[keep, :]
    tail_ref[keep, d_inner:d_inner + nbc] = b_ref[keep, :]
    tail_ref[keep, d_inner + nbc:d_inner + 2 * nbc] = c_ref[keep, :]


def _ssd_mixer(proj, dt_raw, conv_w, conv_b, dt_bias_pad, a_log_pad, d_exp, gnorm_w, batch, seq,
               d_inner):
    t = proj.shape[0]
    chunk = SSD_CHUNK
    nc = seq // chunk
    nbc = SSD_N_GROUPS * SSD_D_STATE
    conv_dim = d_inner + 2 * nbc
    gw = d_inner // SSD_N_GROUPS
    heads_per_group = gw // SSD_HEAD_DIM
    assert heads_per_group == SUBLANES and d_inner % nbc == 0
    assert proj.dtype == BF16
    hh = jnp.arange(LANES)[:, None]
    cc = jnp.arange(SSD_N_GROUPS * LANES)[None, :]
    spread = ((cc % LANES < heads_per_group)
              & (hh == (cc // LANES) * heads_per_group + cc % LANES)).astype(BF16)
    rr = jnp.arange(SSD_D_CONV * chunk)[:, None]
    shift = (jnp.arange(2 * chunk)[None, :] == chunk + rr % chunk - rr // chunk).astype(BF16)
    row = lambda b, c: b * nc + c
    blocks = [2 * _nbytes((chunk, d_inner), proj.dtype), 2 * _nbytes((chunk, nbc), proj.dtype),
              _nbytes((chunk, LANES), F32), _nbytes(spread.shape, BF16), _nbytes(shift.shape, BF16),
              _nbytes((8, conv_dim), F32) * 2, _nbytes((chunk, d_inner), BF16)]
    scratch = (_nbytes((chunk, conv_dim), BF16) + _nbytes((SSD_N_GROUPS, SSD_D_STATE, gw), F32)
               + 2 * _nbytes((chunk, SSD_N_GROUPS * LANES), F32) + 2 * _nbytes((LANES, chunk), F32))
    full = lambda shape: pl.BlockSpec(shape, lambda b, c: (0, 0))
    return pl.pallas_call(
        functools.partial(_ssd_kernel, d_inner=d_inner),
        grid=(batch, nc),
        in_specs=[
            pl.BlockSpec((chunk, d_inner), lambda b, c: (row(b, c), 0)),
            pl.BlockSpec((chunk, d_inner), lambda b, c: (row(b, c), 1)),
            pl.BlockSpec((chunk, nbc), lambda b, c: (row(b, c), 2 * d_inner // nbc)),
            pl.BlockSpec((chunk, nbc), lambda b, c: (row(b, c), 2 * d_inner // nbc + 1)),
            pl.BlockSpec((chunk, LANES), lambda b, c: (row(b, c), 0)),
            full(spread.shape),
            full(shift.shape),
            full((SSD_D_CONV, conv_dim)),
            full((1, conv_dim)),
            full((1, LANES)),
            full((1, LANES)),
            full((1, d_inner)),
            full((1, d_inner)),
        ],
        out_specs=pl.BlockSpec((chunk, d_inner), lambda b, c: (row(b, c), 0)),
        out_shape=jax.ShapeDtypeStruct((t, d_inner), BF16),
        scratch_shapes=[
            pltpu.VMEM((chunk, conv_dim), proj.dtype),
            pltpu.VMEM((SSD_N_GROUPS, SSD_D_STATE, gw), F32),
            pltpu.VMEM((chunk, SSD_N_GROUPS * LANES), F32),
            pltpu.VMEM((chunk, SSD_N_GROUPS * LANES), F32),
            pltpu.VMEM((LANES, chunk), F32),
            pltpu.VMEM((LANES, chunk), F32),
        ],
        compiler_params=pltpu.CompilerParams(
            dimension_semantics=("parallel", "arbitrary"),
            vmem_limit_bytes=_vmem_limit(blocks, scratch)),
        name="ssd_mixer",
    )(proj, proj, proj, proj, dt_raw, spread, shift, conv_w, conv_b.reshape(1, conv_dim),
      dt_bias_pad, a_log_pad, d_exp, gnorm_w.reshape(1, d_inner))


def _pad_lanes(v):
    return jnp.pad(v.astype(F32), (0, LANES - v.shape[0])).reshape(1, LANES)


def _ssd_mix(h, nw, in_w, dt_w, layer, conv_w, conv_b, dt_bias, a_log, d_skip, gnorm_w, d_inner,
             batch, seq):
    main = 2 * d_inner + 2 * SSD_N_GROUPS * SSD_D_STATE
    proj, dt_raw = _ssd_in_proj(h, nw, in_w, dt_w, layer, main)
    d_exp = jnp.repeat(d_skip.astype(F32), SSD_HEAD_DIM).reshape(1, d_inner)
    return _ssd_mixer(proj, dt_raw, conv_w, conv_b, _pad_lanes(dt_bias), _pad_lanes(a_log), d_exp,
                      gnorm_w, batch, seq, d_inner)


def kernel(x, p, norm_w, ssd_in_w, ssd_conv_w, ssd_conv_b, ssd_dt_bias, ssd_a_log, ssd_d, ssd_gnorm_w, ssd_out_w, sb_in_w, sb_qn_w, sb_kn_w, sb_out_w, ple_norm_w, ple_gate_w, ple_proj_w):
    batch, seq, d_model = x.shape
    depth = p.shape[0]
    d_inner = ssd_out_w.shape[1]
    main = 2 * d_inner + 2 * SSD_N_GROUPS * SSD_D_STATE
    n_ssd_heads = ssd_in_w.shape[2] - main
    ssd_in_b = ssd_in_w.astype(BF16)
    ssd_dt_b = jnp.pad(ssd_in_w[:, :, main:], ((0, 0), (0, 0), (0, LANES - n_ssd_heads))).astype(BF16)
    ssd_out_b = ssd_out_w.astype(BF16)
    sb_in_b = sb_in_w.astype(BF16)
    sb_out_b = sb_out_w.astype(BF16)
    gate_b = ple_gate_w.astype(BF16)
    proj_b = ple_proj_w.astype(BF16)
    p_all = p.reshape(depth * batch * seq, p.shape[-1])
    h = x.reshape(batch * seq, d_model)
    for i in range(depth):
        j = i // 2
        if i % 2 == 0:
            a = _ssd_mix(h, norm_w[i], ssd_in_b, ssd_dt_b, j, ssd_conv_w[j], ssd_conv_b[j],
                         ssd_dt_bias[j], ssd_a_log[j], ssd_d[j], ssd_gnorm_w[j], d_inner, batch, seq)
            wo = ssd_out_b
        else:
            qkvg = _sb_in_proj(h, norm_w[i], sb_in_b, j, sb_qn_w[j], sb_kn_w[j])
            a = _sb_attention(qkvg, batch, seq)
            wo = sb_out_b
        h = _out_ple(a, wo, j, h, ple_norm_w[i], gate_b, p_all, proj_b, i)
    return h.reshape(batch, seq, d_model)
```
